```python
import math
import jax
import jax.numpy as jnp
from jax import lax
import numpy as np

D_MODEL = 1024
BATCH = 4
SEQ = 4096
DEPTH = 4

N_META = 16
NORM_EPS = 1e-6
CHUNK = 128
CONV_K = 4
N_BRANCH = 4

RWKV_WIDTH = D_MODEL
RWKV_HEAD = 64
RWKV_HEADS = RWKV_WIDTH // RWKV_HEAD
RWKV_W_RANK = 64
RWKV_A_RANK = 64
RWKV_G_RANK = 128
RWKV_V_RANK = 32
RWKV_LN_EPS = 64e-5

SSM_WIDTH = D_MODEL
SSM_HEAD = 64
SSM_HEADS = SSM_WIDTH // SSM_HEAD
SSM_GROUPS = 4
SSM_STATE = 128

RET_HEADS = 8
RET_QK_WIDTH = D_MODEL // 2
RET_V_WIDTH = D_MODEL
RET_QK_HEAD = RET_QK_WIDTH // RET_HEADS
RET_V_HEAD = RET_V_WIDTH // RET_HEADS
ROPE_BASE = 10000.0

LRU_WIDTH = D_MODEL
LRU_BLOCKS = 8
LRU_BLOCK = LRU_WIDTH // LRU_BLOCKS
LRU_C = 8.0

FFN_HIDDEN = -(-8 * D_MODEL // (3 * 256)) * 256

RWKV_SIZES = (RWKV_WIDTH, RWKV_WIDTH, RWKV_WIDTH, RWKV_W_RANK, RWKV_A_RANK, RWKV_G_RANK)
SSM_XBC = SSM_WIDTH + 2 * SSM_GROUPS * SSM_STATE
SSM_SIZES = (SSM_WIDTH, SSM_XBC, SSM_HEADS)
RET_SIZES = (RET_QK_WIDTH, RET_QK_WIDTH, RET_V_WIDTH, RET_V_WIDTH)
LRU_SIZES = (LRU_WIDTH, LRU_WIDTH)
GROUP_SIZES = (sum(RWKV_SIZES), sum(SSM_SIZES), sum(RET_SIZES), sum(LRU_SIZES), N_BRANCH * D_MODEL)
IN_WIDTH = sum(GROUP_SIZES)

kernel_name = "hybrid_rwkv7_ssd_retention_rglru_trunk"


def _split(p, sizes):
    return jnp.split(p, np.cumsum(sizes)[:-1].tolist(), axis=-1)


def rms_norm(x, w, eps=NORM_EPS):
    x32 = x.astype(jnp.float32)
    y = x32 * lax.rsqrt(jnp.mean(x32 * x32, axis=-1, keepdims=True) + eps)
    return (y * w.astype(jnp.float32)).astype(x.dtype)


def causal_conv(x, w, b):
    y = lax.conv_general_dilated(x, w[:, None, :].astype(x.dtype), window_strides=(1,),
                                 padding=[(CONV_K - 1, 0)], dimension_numbers=('NWC', 'WIO', 'NWC'),
                                 feature_group_count=x.shape[-1])
    return y + b


def token_shift(p):
    return jnp.pad(p, ((0, 0), (1, 0), (0, 0)))[:, :-1]


def pad_front(t, n):
    return jnp.pad(t, ((0, 0), (n, 0)) + ((0, 0),) * (t.ndim - 2))


def segsum(a):
    n = a.shape[-1]
    cs = jnp.cumsum(a, axis=-1)
    diff = cs[..., :, None] - cs[..., None, :]
    return jnp.where(jnp.tril(jnp.ones((n, n), bool)), diff, -jnp.inf)


def rope(x, pos):
    half = x.shape[-1] // 2
    freq = jnp.power(ROPE_BASE, -jnp.arange(half, dtype=jnp.float32) / half)
    ang = pos[:, None] * freq[None, :]
    cos, sin = jnp.cos(ang)[None, :, None, :], jnp.sin(ang)[None, :, None, :]
    x1, x2 = x[..., :half], x[..., half:]
    return jnp.concatenate([x1 * cos - x2 * sin, x2 * cos + x1 * sin], axis=-1)


def rwkv7_mix(p, mu, w2, a2, g2, vec, r_k, v_first, vres):
    p = p.astype(jnp.float32)
    bsz, t_len, _ = p.shape
    p = p + (token_shift(p) - p) * mu
    r, k, v, wd, ad, gd = _split(p, RWKV_SIZES)
    w0, a0, k_k, k_a, ln_w, ln_b = vec
    w_log = -jax.nn.softplus(-(w0 + jnp.tanh(wd) @ w2)) - 0.5
    decay = jnp.exp(-jnp.exp(w_log))
    a = jax.nn.sigmoid(a0 + ad @ a2)
    g = jax.nn.sigmoid(gd) @ g2
    v_own = v
    if vres is not None:
        v0, v1, v2 = vres
        v = v + (v_first - v) * jax.nn.sigmoid(v0 + (v @ v1) @ v2)

    def hd(t):
        return t.reshape(bsz, t_len, RWKV_HEADS, RWKV_HEAD)

    kk = hd(k * k_k)
    kk = kk / jnp.maximum(jnp.sqrt(jnp.sum(kk * kk, axis=-1, keepdims=True)), 1e-12)
    k = k * (1.0 + (a - 1.0) * k_a)
    r4, w4, k4, v4, a4 = hd(r), hd(decay), hd(k), hd(v), hd(a)

    def step(s, inp):
        r_t, w_t, k_t, v_t, a_t, b_t = inp
        sa = jnp.einsum('bhvk,bhk->bhv', s, a_t)
        s = s * w_t[:, :, None, :] + sa[..., None] * b_t[:, :, None, :] + v_t[..., None] * k_t[:, :, None, :]
        return s, jnp.einsum('bhvk,bhk->bhv', s, r_t)

    s0 = jnp.zeros((bsz, RWKV_HEADS, RWKV_HEAD, RWKV_HEAD), jnp.float32)
    xs = tuple(jnp.moveaxis(t, 1, 0) for t in (r4, w4, k4, v4, -kk, kk * a4))
    _, y = lax.scan(step, s0, xs)
    y = jnp.moveaxis(y, 0, 1)
    mean = jnp.mean(y, axis=-1, keepdims=True)
    var = jnp.mean(jnp.square(y - mean), axis=-1, keepdims=True)
    y = ((y - mean) * lax.rsqrt(var + RWKV_LN_EPS)).reshape(bsz, t_len, RWKV_WIDTH) * ln_w + ln_b
    bonus = jnp.sum(r4 * k4 * r_k, axis=-1, keepdims=True) * v4
    return (y + bonus.reshape(bsz, t_len, RWKV_WIDTH)) * g, v_own


def ssd_mix(p, conv_w, conv_b, dt_bias, a_log, d_skip, norm_w):
    p = p.astype(jnp.float32)
    bsz, t_len, _ = p.shape
    G, E, P, N, L = SSM_GROUPS, SSM_HEADS // SSM_GROUPS, SSM_HEAD, SSM_STATE, CHUNK
    z, xbc, dt = _split(p, SSM_SIZES)
    xbc = jax.nn.silu(causal_conv(xbc, conv_w, conv_b))
    xs, bm, cm = _split(xbc, (SSM_WIDTH, G * N, G * N))
    dt = jax.nn.softplus(dt + dt_bias)
    A = -jnp.exp(a_log.astype(jnp.float32))
    pad = (-t_len) % L
    nc = (t_len + pad) // L

    def chunk(t, *tail):
        return pad_front(t, pad).reshape(bsz, nc, L, *tail)

    x6 = chunk(xs, G, E, P)
    dt5 = chunk(dt, G, E)
    bc = chunk(bm, G, N)
    cc = chunk(cm, G, N)
    dA = jnp.transpose(dt5 * A.reshape(G, E), (0, 3, 4, 1, 2))
    a_cum = jnp.cumsum(dA, axis=-1)
    lmat = jnp.exp(segsum(dA))
    xdt = x6 * dt5[..., None]
    cb = jnp.einsum('bclgn,bcsgn->bcgls', cc, bc)
    y_diag = jnp.einsum('bcgls,bgecls,bcsgep->bclgep', cb, lmat, xdt)
    decay_states = jnp.exp(a_cum[..., -1:] - a_cum)
    states = jnp.einsum('bclgn,bgecl,bclgep->bcgepn', bc, decay_states, xdt)
    states = jnp.concatenate([jnp.zeros_like(states[:, :1]), states], axis=1)
    chunk_decay = jnp.exp(segsum(jnp.pad(a_cum[..., -1], ((0, 0), (0, 0), (0, 0), (1, 0)))))
    states = jnp.einsum('bgezc,bcgepn->bzgepn', chunk_decay, states)[:, :-1]
    y_off = jnp.einsum('bclgn,bcgepn,bgecl->bclgep', cc, states, jnp.exp(a_cum))
    y = y_diag + y_off + x6 * d_skip.reshape(G, E)[..., None]
    y = y.reshape(bsz, nc * L, SSM_WIDTH)[:, pad:] * jax.nn.silu(z)
    yg = y.reshape(bsz, t_len, G, SSM_WIDTH // G)
    yg = yg * lax.rsqrt(jnp.mean(yg * yg, axis=-1, keepdims=True) + 1e-5)
    return yg.reshape(bsz, t_len, SSM_WIDTH) * norm_w


def retention_mix(p):
    p = p.astype(jnp.float32)
    bsz, t_len, _ = p.shape
    H, DK, DV, L = RET_HEADS, RET_QK_HEAD, RET_V_HEAD, CHUNK
    q, k, v, g = _split(p, RET_SIZES)
    pos = jnp.arange(t_len, dtype=jnp.float32)
    q = rope(q.reshape(bsz, t_len, H, DK), pos)
    k = rope(k.reshape(bsz, t_len, H, DK), pos) * (DK ** -0.5)
    v = v.reshape(bsz, t_len, H, DV)
    pad = (-t_len) % L
    nc = (t_len + pad) // L
    qc = pad_front(q, pad).reshape(bsz, nc, L, H, DK)
    kc = pad_front(k, pad).reshape(bsz, nc, L, H, DK)
    vc = pad_front(v, pad).reshape(bsz, nc, L, H, DV)
    log_g = jnp.log1p(-jnp.exp2(-5.0 - jnp.arange(H, dtype=jnp.float32)))
    idx = jnp.arange(L, dtype=jnp.float32)
    rel = idx[:, None] - idx[None, :]
    causal = rel >= 0
    intra = jnp.where(causal, jnp.exp(jnp.where(causal, rel, 0.0)[None] * log_g[:, None, None]), 0.0)
    scores = jnp.einsum('bclhd,bcshd->bchls', qc, kc) * intra
    y_in = jnp.einsum('bchls,bcshe->bclhe', scores, vc)
    k_dec = jnp.exp((L - 1.0 - idx)[None, :] * log_g[:, None])
    q_dec = jnp.exp((idx + 1.0)[None, :] * log_g[:, None])
    chunk_dec = jnp.exp(L * log_g)[None, :, None, None]
    kv = jnp.einsum('bclhd,hl,bclhe->bchde', kc, k_dec, vc)

    def step(r_state, kv_c):
        return chunk_dec * r_state + kv_c, r_state

    _, r_prev = lax.scan(step, jnp.zeros((bsz, H, DK, DV), jnp.float32), jnp.moveaxis(kv, 1, 0))
    y_x = jnp.einsum('bclhd,hl,cbhde->bclhe', qc, q_dec, r_prev)
    y = (y_in + y_x).reshape(bsz, nc * L, H, DV)[:, pad:]
    y = y * lax.rsqrt(jnp.mean(y * y, axis=-1, keepdims=True) + NORM_EPS)
    return jax.nn.silu(g) * y.reshape(bsz, t_len, RET_V_WIDTH)


def rglru_mix(p, conv_w, conv_b, w_gate, b_gate, lam):
    p = p.astype(jnp.float32)
    bsz, t_len, _ = p.shape
    y_in, x_in = _split(p, LRU_SIZES)
    gate = jax.nn.gelu(y_in)
    xc = causal_conv(x_in, conv_w, conv_b)
    xb = xc.reshape(bsz, t_len, LRU_BLOCKS, LRU_BLOCK)
    gates = jnp.einsum('btnc,knce->kbtne', xb, w_gate).reshape(2, bsz, t_len, LRU_WIDTH) + b_gate[:, None, None, :]
    r_gate = jax.nn.sigmoid(gates[0])
    i_gate = jax.nn.sigmoid(gates[1])
    log_a = -LRU_C * r_gate * jax.nn.softplus(-lam)
    a = jnp.exp(log_a)
    u = jnp.sqrt(-jnp.expm1(2.0 * log_a)) * (i_gate * xc)

    def combine(left, right):
        a1, b1 = left
        a2, b2 = right
        return a1 * a2, a2 * b1 + b2

    _, h = lax.associative_scan(combine, (a, u), axis=1)
    return h * gate


def setup_inputs(seed: int = 0) -> dict:
    key = jax.random.key(seed)
    keys = iter(jax.random.split(key, 64))
    f32 = jnp.float32

    def nrm(shape, scale):
        return jax.random.normal(next(keys), shape, f32) * scale

    def uni(shape, lo, hi):
        return jax.random.uniform(next(keys), shape, f32, lo, hi)

    L = DEPTH
    W = RWKV_WIDTH
    dt0 = jnp.exp(uni((L, SSM_HEADS), math.log(1e-3), math.log(1e-1)))
    a_pow = uni((L, LRU_WIDTH), 0.9, 0.999) ** (1.0 / LRU_C)
    rwkv_vec = jnp.stack([uni((L, W), -6.0, -1.0), nrm((L, W), 0.1), 0.85 + nrm((L, W), 0.02),
                          1.0 + nrm((L, W), 0.02), 1.0 + nrm((L, W), 0.02), nrm((L, W), 0.02)], axis=1)
    return {
        "x": nrm((BATCH, SEQ, D_MODEL), 1.0),
        "meta": nrm((N_META, D_MODEL), 1.0),
        "norm_mix": 1.0 + nrm((L, D_MODEL), 0.02),
        "norm_ffn": 1.0 + nrm((L, D_MODEL), 0.02),
        "w_in": nrm((L, D_MODEL, IN_WIDTH), D_MODEL ** -0.5),
        "rwkv_mu": uni((L, sum(RWKV_SIZES)), 0.0, 1.0),
        "rwkv_w2": nrm((L, RWKV_W_RANK, W), 0.1),
        "rwkv_a2": nrm((L, RWKV_A_RANK, W), RWKV_A_RANK ** -0.5),
        "rwkv_g2": nrm((L, RWKV_G_RANK, W), RWKV_G_RANK ** -0.5),
        "rwkv_vec": rwkv_vec,
        "rwkv_rk": nrm((L, RWKV_HEADS, RWKV_HEAD), 0.1),
        "rwkv_v0": nrm((L - 1, W), 0.1),
        "rwkv_v1": nrm((L - 1, W, RWKV_V_RANK), W ** -0.5),
        "rwkv_v2": nrm((L - 1, RWKV_V_RANK, W), RWKV_V_RANK ** -0.5),
        "ssm_conv_w": nrm((L, CONV_K, SSM_XBC), CONV_K ** -0.5),
        "ssm_conv_b": nrm((L, SSM_XBC), 0.02),
        "ssm_dt_bias": dt0 + jnp.log(-jnp.expm1(-dt0)),
        "ssm_a_log": jnp.log(uni((L, SSM_HEADS), 1.0, 16.0)),
        "ssm_d": 1.0 + nrm((L, SSM_HEADS), 0.02),
        "ssm_norm": 1.0 + nrm((L, SSM_WIDTH), 0.02),
        "lru_conv_w": nrm((L, CONV_K, LRU_WIDTH), CONV_K ** -0.5),
        "lru_conv_b": nrm((L, LRU_WIDTH), 0.02),
        "lru_w_gate": nrm((L, 2, LRU_BLOCKS, LRU_BLOCK, LRU_BLOCK), LRU_BLOCK ** -0.5),
        "lru_b_gate": nrm((L, 2, LRU_WIDTH), 0.02),
        "lru_lambda": jnp.log(a_pow) - jnp.log1p(-a_pow),
        "w_branch": nrm((L, N_BRANCH, D_MODEL, D_MODEL), D_MODEL ** -0.5),
        "w_out": nrm((L, D_MODEL, D_MODEL), D_MODEL ** -0.5),
        "w_ffn_in": nrm((L, D_MODEL, 2 * FFN_HIDDEN), D_MODEL ** -0.5),
        "w_ffn_out": nrm((L, FFN_HIDDEN, D_MODEL), FFN_HIDDEN ** -0.5),
        "final_norm": 1.0 + nrm((D_MODEL,), 0.02),
    }


def reference(x, meta, norm_mix, norm_ffn, w_in, rwkv_mu, rwkv_w2, rwkv_a2, rwkv_g2, rwkv_vec, rwkv_rk,
              rwkv_v0, rwkv_v1, rwkv_v2, ssm_conv_w, ssm_conv_b, ssm_dt_bias, ssm_a_log, ssm_d, ssm_norm,
              lru_conv_w, lru_conv_b, lru_w_gate, lru_b_gate, lru_lambda, w_branch, w_out, w_ffn_in,
              w_ffn_out, final_norm):
    bsz = x.shape[0]
    s = jnp.concatenate([jnp.broadcast_to(meta[None].astype(x.dtype), (bsz, N_META, D_MODEL)), x], axis=1)
    t_len = s.shape[1]
    v_first = None
    for l in range(DEPTH):
        hn = rms_norm(s, norm_mix[l])
        p_a, p_b, p_c, p_d, p_g = _split(hn @ w_in[l], GROUP_SIZES)
        vres = None if l == 0 else (rwkv_v0[l - 1], rwkv_v1[l - 1], rwkv_v2[l - 1])
        y_a, v_a = rwkv7_mix(p_a, rwkv_mu[l], rwkv_w2[l], rwkv_a2[l], rwkv_g2[l], rwkv_vec[l], rwkv_rk[l],
                             v_first, vres)
        if l == 0:
            v_first = v_a
        y_b = ssd_mix(p_b, ssm_conv_w[l], ssm_conv_b[l], ssm_dt_bias[l], ssm_a_log[l], ssm_d[l], ssm_norm[l])
        y_c = retention_mix(p_c)
        y_d = rglru_mix(p_d, lru_conv_w[l], lru_conv_b[l], lru_w_gate[l], lru_b_gate[l], lru_lambda[l])
        ys = jnp.stack([y_a, y_b, y_c, y_d], axis=2).astype(s.dtype)
        zb = jnp.einsum('btnc,ncd->btnd', ys, w_branch[l])
        gate = jax.nn.sigmoid(p_g.reshape(bsz, t_len, N_BRANCH, D_MODEL))
        s = s + jnp.sum(gate * zb, axis=2) @ w_out[l]
        hn = rms_norm(s, norm_ffn[l])
        g_in, u_in = jnp.split(hn @ w_ffn_in[l], 2, axis=-1)
        s = s + (jax.nn.silu(g_in) * u_in) @ w_ffn_out[l]
    return rms_norm(s, final_norm)[:, N_META:]
```

```python
import functools
import math

import jax
import jax.numpy as jnp
import numpy as np
from jax import lax
from jax.experimental import pallas as pl
from jax.experimental.pallas import tpu as pltpu

D = 1024
N_META = 16
NORM_EPS = 1e-6
CONV_K = 4
N_BRANCH = 4
CHUNK = 128
RWKV_CHUNK = 64
HEAD = 64
RWKV_LN_EPS = 64e-5
SSM_GROUPS = 4
SSM_STATE = 128
SSM_HEADS = 16
RET_HEADS = 8
RET_DK = 64
RET_DV = 128
ROPE_BASE = 10000.0
LRU_BLOCKS = 8
LRU_C = 8.0
FFN_HIDDEN = 2816
LANES = 128
SUBLANES = 8
VMEM_LIMIT = 56 * 1024 * 1024

COL_GATE = 0
COL_LRU = 4096
COL_XBC = 6144
COL_Z = 8192
COL_RET_QK = 9216
COL_RET_V = 10240
COL_RET_G = 11264
COL_RWKV = 12288
COL_RWKV_LR = 15360
COL_DT = 15616
IN_COLS = 15872

BF16 = jnp.bfloat16
F32 = jnp.float32


def _row_tile(tp, target, multiple=SUBLANES):
    best = multiple
    for t in range(multiple, target + 1, multiple):
        if tp % t == 0:
            best = t
    return best


def _cp(*sem):
    return pltpu.CompilerParams(dimension_semantics=sem, vmem_limit_bytes=VMEM_LIMIT)


def _iota(shape, dim):
    return lax.broadcasted_iota(jnp.int32, shape, dim)


def _dot(a, b):
    return jnp.dot(a.astype(BF16), b.astype(BF16), preferred_element_type=F32)


def _dot_nt(a, b):
    return lax.dot_general(a.astype(BF16), b.astype(BF16), (((1,), (1,)), ((), ())), preferred_element_type=F32)


def _dot_tn(a, b):
    return lax.dot_general(a.astype(BF16), b.astype(BF16), (((0,), (0,)), ((), ())), preferred_element_type=F32)


def _dot_exact(a, b):
    return jnp.dot(a, b, preferred_element_type=F32, precision=lax.Precision.HIGHEST)


def _softplus(x):
    return jnp.maximum(x, 0.0) + jnp.log1p(jnp.exp(-jnp.abs(x)))


def _silu(x):
    return x * jax.nn.sigmoid(x)


def _shift_rows(x, d, tail):
    xs = pltpu.roll(x, d, 0)
    ts = pltpu.roll(tail, d, 0)
    head = jnp.where(_iota(ts.shape, 0) < d, ts, xs[:SUBLANES])
    return jnp.concatenate([head, xs[SUBLANES:]], axis=0)


def _causal_conv(x, tail, w_ref, b_ref):
    acc = b_ref[...] + x * w_ref[CONV_K - 1:CONV_K, :]
    for d in range(1, CONV_K):
        acc = acc + _shift_rows(x, d, tail) * w_ref[CONV_K - 1 - d:CONV_K - d, :]
    return acc


def _cumsum_rows(x):
    row = _iota(x.shape, 0)
    d = 1
    while d < x.shape[0]:
        x = x + jnp.where(row >= d, pltpu.roll(x, d, 0), 0.0)
        d *= 2
    return x


def _valid_rows(shape, chunk_idx, pad):
    return _iota(shape, 0) + chunk_idx * shape[0] >= pad


def _rms_norm_to(hn_ref, x_ref, nw_ref):
    tm = x_ref.shape[0]
    sub = _row_tile(tm, 512, 2 * SUBLANES)

    def body(i, carry):
        rows = pl.ds(pl.multiple_of(i * sub, 2 * SUBLANES), sub)
        x = x_ref[rows, :]
        y = x * lax.rsqrt(jnp.mean(x * x, axis=-1, keepdims=True) + NORM_EPS)
        hn_ref[rows, :] = (y * nw_ref[...]).astype(BF16)
        return carry

    lax.fori_loop(0, tm // sub, body, 0)


def _inproj_kernel(x_ref, nw_ref, w_ref, o_ref, hn_ref):
    @pl.when(pl.program_id(2) == 0)
    def _():
        _rms_norm_to(hn_ref, x_ref, nw_ref)

    o_ref[...] = jnp.dot(hn_ref[...], w_ref[...], preferred_element_type=F32)


def _inproj(s, norm_w, w, tm, tn):
    b, tp, d = s.shape
    n = w.shape[1]
    return pl.pallas_call(
        _inproj_kernel,
        grid=(b, tp // tm, n // tn),
        in_specs=[
            pl.BlockSpec((None, tm, d), lambda bi, i, j: (bi, i, 0)),
            pl.BlockSpec((1, d), lambda bi, i, j: (0, 0)),
            pl.BlockSpec((d, tn), lambda bi, i, j: (0, j)),
        ],
        out_specs=pl.BlockSpec((None, tm, tn), lambda bi, i, j: (bi, i, j)),
        out_shape=jax.ShapeDtypeStruct((b, tp, n), F32),
        scratch_shapes=[pltpu.VMEM((tm, d), BF16)],
        compiler_params=_cp("parallel", "parallel", "arbitrary"),
        name="inproj",
    )(s, norm_w, w)


def _lru_kernel(pad, y_ref, x_ref, cw_ref, cb_ref, wg_ref, bg_ref, lam_ref, o_ref, tail_ref, h_ref):
    c = pl.program_id(1)

    @pl.when(c == 0)
    def _():
        tail_ref[...] = jnp.zeros_like(tail_ref)
        h_ref[...] = jnp.zeros_like(h_ref)

    x = x_ref[...]
    n_rows = x.shape[0]
    xc = _causal_conv(x, tail_ref[...], cw_ref, cb_ref)
    tail_ref[...] = x[n_rows - SUBLANES:]
    xcb = xc.astype(BF16)
    blk = D // LRU_BLOCKS

    def gates(k):
        parts = [jnp.dot(xcb[:, n * blk:(n + 1) * blk], wg_ref[k, n], preferred_element_type=F32)
                 for n in range(LRU_BLOCKS)]
        return jax.nn.sigmoid(jnp.concatenate(parts, axis=1) + bg_ref[k:k + 1, :])

    r_gate = gates(0)
    i_gate = gates(1)
    log_a = -LRU_C * r_gate * _softplus(-lam_ref[...])
    a = jnp.exp(log_a)
    u = jnp.sqrt(1.0 - jnp.exp(2.0 * log_a)) * (i_gate * xc)
    u = jnp.where(_valid_rows(u.shape, c, pad), u, 0.0)
    row = _iota(a.shape, 0)
    d = 1
    while d < n_rows:
        keep = row >= d
        a_prev = jnp.where(keep, pltpu.roll(a, d, 0), 1.0)
        u_prev = jnp.where(keep, pltpu.roll(u, d, 0), 0.0)
        u = a * u_prev + u
        a = a * a_prev
        d *= 2
    h = u + a * h_ref[SUBLANES - 1:SUBLANES, :]
    h_ref[...] = h[n_rows - SUBLANES:]
    o_ref[...] = (h * jax.nn.gelu(y_ref[...])).astype(o_ref.dtype)


def _lru_mix(p_all, pad, cw, cb, wg, bg, lam):
    b, tp, _ = p_all.shape
    full = lambda a: pl.BlockSpec(a.shape, lambda bi, c: (0,) * a.ndim)
    return pl.pallas_call(
        functools.partial(_lru_kernel, pad),
        grid=(b, tp // CHUNK),
        in_specs=[
            pl.BlockSpec((None, CHUNK, D), lambda bi, c: (bi, c, COL_LRU // D)),
            pl.BlockSpec((None, CHUNK, D), lambda bi, c: (bi, c, COL_LRU // D + 1)),
            full(cw), full(cb), full(wg), full(bg), full(lam),
        ],
        out_specs=pl.BlockSpec((None, CHUNK, D), lambda bi, c: (bi, c, 0)),
        out_shape=jax.ShapeDtypeStruct((b, tp, D), BF16),
        scratch_shapes=[pltpu.VMEM((SUBLANES, D), F32), pltpu.VMEM((SUBLANES, D), F32)],
        compiler_params=_cp("parallel", "arbitrary"),
        name="rglru",
    )(p_all, p_all, cw, cb, wg, bg, lam)


def _ssd_kernel(pad, xbc_ref, z_ref, dt_ref, cw_ref, cb_ref, dtb_ref, alog_ref, dskip_ref, nw_ref,
                o_ref, tail_ref, state_ref):
    c = pl.program_id(1)

    @pl.when(c == 0)
    def _():
        tail_ref[...] = jnp.zeros_like(tail_ref)
        state_ref[...] = jnp.zeros_like(state_ref)

    raw = xbc_ref[...]
    n_rows = raw.shape[0]
    xbc = _silu(_causal_conv(raw, tail_ref[...], cw_ref, cb_ref))
    tail_ref[...] = raw[n_rows - SUBLANES:]
    gn = SSM_GROUPS * SSM_STATE
    xs = jnp.where(_valid_rows((n_rows, D), c, pad), xbc[:, :D], 0.0)
    bm = xbc[:, D:D + gn]
    cm = xbc[:, D + gn:]

    dt = _softplus(dt_ref[...] + dtb_ref[...])
    a_cum = _cumsum_rows(dt * -jnp.exp(alog_ref[...]))
    a_last = a_cum[n_rows - 1:n_rows, :]
    expand = (_iota((LANES, D), 1) // HEAD == _iota((LANES, D), 0)).astype(F32)
    dt_x = _dot_exact(dt, expand)
    acum_x = _dot_exact(a_cum, expand)
    alast_x = _dot_exact(jnp.broadcast_to(a_last, (SUBLANES, LANES)), expand)[:1]
    xdt = xs * dt_x
    xdt_end = xdt * jnp.exp(alast_x - acum_x)
    a_cum_t = a_cum.T

    tril = _iota((n_rows, n_rows), 0) >= _iota((n_rows, n_rows), 1)
    lane_lo = _iota((n_rows, LANES), 1) < HEAD
    heads_per_group = SSM_HEADS // SSM_GROUPS
    gw = heads_per_group * HEAD
    outs = []
    for g in range(SSM_GROUPS):
        b_g = bm[:, g * SSM_STATE:(g + 1) * SSM_STATE]
        c_g = cm[:, g * SSM_STATE:(g + 1) * SSM_STATE]
        cb = _dot_nt(c_g, b_g)
        s_prev = state_ref[g]
        y_off = _dot(c_g, s_prev) * jnp.exp(acum_x[:, g * gw:(g + 1) * gw])
        state_ref[g] = s_prev * jnp.exp(alast_x[:, g * gw:(g + 1) * gw]) + _dot_tn(b_g, xdt_end[:, g * gw:(g + 1) * gw])
        y_diag = []
        for pr in range(heads_per_group // 2):
            col = g * gw + pr * LANES
            x_pair = xdt[:, col:col + LANES]
            acc = None
            for e in range(2):
                h = (col // HEAD) + e
                seg = jnp.where(tril, a_cum[:, h:h + 1] - a_cum_t[h:h + 1, :], -1e30)
                m = cb * jnp.exp(seg)
                x_h = jnp.where(lane_lo if e == 0 else ~lane_lo, x_pair, 0.0)
                t = _dot(m, x_h)
                acc = t if acc is None else acc + t
            y_diag.append(acc)
        outs.append(jnp.concatenate(y_diag, axis=1) + y_off)
    y = jnp.concatenate(outs, axis=1) + xs * dskip_ref[...]
    y = y * _silu(z_ref[...])
    normed = []
    for g in range(SSM_GROUPS):
        yg = y[:, g * gw:(g + 1) * gw]
        normed.append(yg * lax.rsqrt(jnp.mean(yg * yg, axis=-1, keepdims=True) + 1e-5))
    o_ref[...] = (jnp.concatenate(normed, axis=1) * nw_ref[...]).astype(o_ref.dtype)


def _ssd_mix(p_all, pad, cw, cb, dtb, alog, dskip, nw):
    b, tp, _ = p_all.shape
    full = lambda a: pl.BlockSpec(a.shape, lambda bi, c: (0,) * a.ndim)
    return pl.pallas_call(
        functools.partial(_ssd_kernel, pad),
        grid=(b, tp // CHUNK),
        in_specs=[
            pl.BlockSpec((None, CHUNK, 2 * D), lambda bi, c: (bi, c, COL_XBC // (2 * D))),
            pl.BlockSpec((None, CHUNK, D), lambda bi, c: (bi, c, COL_Z // D)),
            pl.BlockSpec((None, CHUNK, LANES), lambda bi, c: (bi, c, COL_DT // LANES)),
            full(cw), full(cb), full(dtb), full(alog), full(dskip), full(nw),
        ],
        out_specs=pl.BlockSpec((None, CHUNK, D), lambda bi, c: (bi, c, 0)),
        out_shape=jax.ShapeDtypeStruct((b, tp, D), BF16),
        scratch_shapes=[pltpu.VMEM((SUBLANES, 2 * D), F32),
                        pltpu.VMEM((SSM_GROUPS, SSM_STATE, D // SSM_GROUPS), F32)],
        compiler_params=_cp("parallel", "arbitrary"),
        name="ssd",
    )(p_all, p_all, p_all, cw, cb, dtb, alog, dskip, nw)


_RET_LOG_G = [math.log1p(-2.0 ** (-5.0 - h)) for h in range(RET_HEADS)]


def _ret_kernel(qk_ref, v_ref, g_ref, cos_ref, sin_ref, o_ref, state_ref):
    c = pl.program_id(1)

    @pl.when(c == 0)
    def _():
        state_ref[...] = jnp.zeros_like(state_ref)

    qk = qk_ref[...]
    n_rows, width = qk.shape
    half = RET_DK // 2
    first = (_iota(qk.shape, 1) % RET_DK) < half
    partner = jnp.where(first, pltpu.roll(qk, width - half, 1), pltpu.roll(qk, half, 1))
    qk = qk * cos_ref[...] + partner * sin_ref[...]
    hw = RET_HEADS * RET_DK
    q = qk[:, :hw]
    k = qk[:, hw:] * (RET_DK ** -0.5)
    v = v_ref[...]

    rowf = _iota((n_rows, LANES), 0).astype(F32)
    rel = (_iota((n_rows, n_rows), 0) - _iota((n_rows, n_rows), 1)).astype(F32)
    lane_lo = _iota((n_rows, LANES), 1) < RET_DK
    outs = []
    for h in range(RET_HEADS):
        log_g = _RET_LOG_G[h]
        col = (h // 2) * LANES
        mask = lane_lo if h % 2 == 0 else ~lane_lo
        q_h = jnp.where(mask, q[:, col:col + LANES], 0.0)
        k_h = jnp.where(mask, k[:, col:col + LANES], 0.0)
        v_h = v[:, h * RET_DV:(h + 1) * RET_DV]
        intra = jnp.where(rel >= 0, jnp.exp(jnp.maximum(rel, 0.0) * log_g), 0.0)
        y_in = _dot(_dot_nt(q_h, k_h) * intra, v_h)
        r_prev = state_ref[h]
        y_x = _dot(q_h, r_prev) * jnp.exp((rowf + 1.0) * log_g)
        k_dec = k_h * jnp.exp((n_rows - 1.0 - rowf) * log_g)
        state_ref[h] = math.exp(n_rows * log_g) * r_prev + _dot_tn(k_dec, v_h)
        y = y_in + y_x
        outs.append(y * lax.rsqrt(jnp.mean(y * y, axis=-1, keepdims=True) + NORM_EPS))
    o_ref[...] = (_silu(g_ref[...]) * jnp.concatenate(outs, axis=1)).astype(o_ref.dtype)


def _ret_mix(p_all, cos_t, sin_t):
    b, tp, _ = p_all.shape
    return pl.pallas_call(
        _ret_kernel,
        grid=(b, tp // CHUNK),
        in_specs=[
            pl.BlockSpec((None, CHUNK, D), lambda bi, c: (bi, c, COL_RET_QK // D)),
            pl.BlockSpec((None, CHUNK, D), lambda bi, c: (bi, c, COL_RET_V // D)),
            pl.BlockSpec((None, CHUNK, D), lambda bi, c: (bi, c, COL_RET_G // D)),
            pl.BlockSpec((CHUNK, D), lambda bi, c: (c, 0)),
            pl.BlockSpec((CHUNK, D), lambda bi, c: (c, 0)),
        ],
        out_specs=pl.BlockSpec((None, CHUNK, D), lambda bi, c: (bi, c, 0)),
        out_shape=jax.ShapeDtypeStruct((b, tp, D), BF16),
        scratch_shapes=[pltpu.VMEM((RET_HEADS, LANES, RET_DV), F32)],
        compiler_params=_cp("parallel", "arbitrary"),
        name="retention",
    )(p_all, p_all, p_all, cos_t, sin_t)


def _head_sum(x, ones_bd):
    parts = [_dot_exact(x[:, j * LANES:(j + 1) * LANES], ones_bd) for j in range(x.shape[1] // LANES)]
    return jnp.concatenate(parts, axis=1)


def _rwkv_kernel(has_vres, *refs):
    if has_vres:
        (r_ref, k_ref, v_ref, lr_ref, mu_ref, mulr_ref, vec_ref, rk_ref, w2_ref, a2_ref, g2_ref,
         vf_ref, v0_ref, v1_ref, v2_ref, o_ref, tail_ref, taillr_ref, state_ref) = refs
    else:
        (r_ref, k_ref, v_ref, lr_ref, mu_ref, mulr_ref, vec_ref, rk_ref, w2_ref, a2_ref, g2_ref,
         o_ref, vown_ref, tail_ref, taillr_ref, state_ref) = refs
    c = pl.program_id(1)

    @pl.when(c == 0)
    def _():
        tail_ref[...] = jnp.zeros_like(tail_ref)
        taillr_ref[...] = jnp.zeros_like(taillr_ref)
        state_ref[...] = jnp.zeros_like(state_ref)

    n_rows = r_ref.shape[0]

    def mixed(ref, tail, mu):
        x = ref[...]
        return x + (_shift_rows(x, 1, tail) - x) * mu, x[n_rows - SUBLANES:]

    r, t0 = mixed(r_ref, tail_ref[0], mu_ref[0:1, :])
    k, t1 = mixed(k_ref, tail_ref[1], mu_ref[1:2, :])
    v, t2 = mixed(v_ref, tail_ref[2], mu_ref[2:3, :])
    lr, t3 = mixed(lr_ref, taillr_ref[...], mulr_ref[...])
    tail_ref[0] = t0
    tail_ref[1] = t1
    tail_ref[2] = t2
    taillr_ref[...] = t3

    w0, a0, k_k, k_a, ln_w, ln_b = (vec_ref[i:i + 1, :] for i in range(6))
    wa = lr[:, :LANES]
    w_log = -_softplus(-(w0 + _dot(jnp.tanh(wa), w2_ref[...]))) - 0.5
    log_decay = -jnp.exp(w_log)
    a_sig = jax.nn.sigmoid(a0 + _dot(wa, a2_ref[...]))
    gate = _dot(jax.nn.sigmoid(lr[:, LANES:]), g2_ref[...])
    if has_vres:
        mixv = jax.nn.sigmoid(v0_ref[...] + _dot(_dot(v, v1_ref[...]), v2_ref[...]))
        v = v + (vf_ref[...] - v) * mixv
    else:
        vown_ref[...] = v

    ones_bd = (_iota((LANES, LANES), 0) // HEAD == _iota((LANES, LANES), 1) // HEAD).astype(F32)
    kk = k * k_k
    kk = kk / jnp.maximum(jnp.sqrt(_head_sum(kk * kk, ones_bd)), 1e-12)
    k = k * (1.0 + (a_sig - 1.0) * k_a)
    a_vec = -kk
    b_vec = kk * a_sig

    cum = _cumsum_rows(log_decay)
    cum_last = cum[n_rows - 1:n_rows, :]
    g_inc = jnp.exp(cum)
    g_inv = jnp.exp(-cum)
    to_end = jnp.exp(cum_last - cum)
    r_t = r * g_inc
    k_t = k * g_inv
    b_t = b_vec * g_inv
    a_t = a_vec * jnp.exp(cum - log_decay)
    k_end = k * to_end
    b_end = b_vec * to_end
    decay_all = jnp.broadcast_to(jnp.exp(cum_last), (LANES, D))

    n4 = 4 * n_rows
    row4 = _iota((n4, 2 * n_rows), 0)
    col4 = _iota((n4, 2 * n_rows), 1)
    score_mask = (row4 % n_rows) + jnp.where(row4 >= 2 * n_rows, 1, 0) > (col4 % n_rows)
    lane_lo = _iota((n_rows, LANES), 1) < HEAD
    eye = (_iota((LANES, LANES), 0) == _iota((LANES, LANES), 1)).astype(F32)
    blockdiag = (_iota((LANES, LANES), 0) // HEAD) == (_iota((LANES, LANES), 1) // HEAD)
    zeros_half = jnp.zeros((n_rows, LANES), F32)

    ys = []
    for j in range(D // LANES):
        sl = slice(j * LANES, (j + 1) * LANES)
        at, rt, kt, bt, vp = a_t[:, sl], r_t[:, sl], k_t[:, sl], b_t[:, sl], v[:, sl]
        lhs = jnp.concatenate([jnp.where(lane_lo, at, 0.0), jnp.where(lane_lo, 0.0, at),
                               jnp.where(lane_lo, rt, 0.0), jnp.where(lane_lo, 0.0, rt)], axis=0)
        m = jnp.where(score_mask, _dot_nt(lhs, jnp.concatenate([kt, bt], axis=0)), 0.0)
        m_a = m[:2 * n_rows]
        a_ab = jnp.where(blockdiag,
                         jnp.concatenate([pltpu.roll(m_a[:n_rows], HEAD, 1), m_a[n_rows:]], axis=0), 0.0)
        inv = eye + a_ab
        power = a_ab
        span = 1
        while 2 * span < n_rows:
            power = _dot(power, power)
            inv = inv + _dot(inv, power)
            span *= 2
        s_prev = state_ref[j]
        proj = _dot(jnp.concatenate([at, rt], axis=0), s_prev)
        z = proj[:n_rows]
        m_ak = jnp.where(_iota(m_a.shape, 1) < n_rows, m_a, 0.0)
        rhs = jnp.concatenate([z, z], axis=0) + _dot(m_ak, jnp.concatenate([vp, zeros_half], axis=0))
        u_st = _dot(inv, jnp.where(blockdiag, rhs, 0.0))
        u0, u1 = u_st[:n_rows], u_st[n_rows:]
        vu = jnp.concatenate([jnp.where(lane_lo, vp, 0.0), u0, jnp.where(lane_lo, 0.0, vp), u1], axis=0)
        m_r = jnp.concatenate([m[2 * n_rows:3 * n_rows], m[3 * n_rows:]], axis=1)
        ys.append(proj[n_rows:] + _dot(m_r, vu))
        upd = _dot_tn(jnp.concatenate([k_end[:, sl], b_end[:, sl]], axis=0),
                      jnp.concatenate([vp, u0 + u1], axis=0))
        state_ref[j] = decay_all[:, sl].T * s_prev + jnp.where(blockdiag, upd, 0.0)
    y = jnp.concatenate(ys, axis=1)

    mean = _head_sum(y, ones_bd) * (1.0 / HEAD)
    yc = y - mean
    var = _head_sum(yc * yc, ones_bd) * (1.0 / HEAD)
    y = yc * lax.rsqrt(var + RWKV_LN_EPS) * ln_w + ln_b
    bonus = _head_sum(r * k * rk_ref[...], ones_bd) * v
    o_ref[...] = ((y + bonus) * gate).astype(o_ref.dtype)


def _rwkv_mix(p_all, mu, mu_lr, vec, rk, w2, a2, g2, vres):
    b, tp, _ = p_all.shape
    tb = RWKV_CHUNK
    full = lambda a: pl.BlockSpec(a.shape, lambda bi, c: (0,) * a.ndim)
    col = lambda off, w: pl.BlockSpec((None, tb, w), lambda bi, c: (bi, c, off // w))
    row_out = pl.BlockSpec((None, tb, D), lambda bi, c: (bi, c, 0))
    in_specs = [col(COL_RWKV, D), col(COL_RWKV + D, D), col(COL_RWKV + 2 * D, D), col(COL_RWKV_LR, 2 * LANES),
                full(mu), full(mu_lr), full(vec), full(rk), full(w2), full(a2), full(g2)]
    args = [p_all, p_all, p_all, p_all, mu, mu_lr, vec, rk, w2, a2, g2]
    scratch = [pltpu.VMEM((3, SUBLANES, D), F32), pltpu.VMEM((SUBLANES, 2 * LANES), F32),
               pltpu.VMEM((D // LANES, LANES, LANES), F32)]
    if vres is None:
        out_specs = [row_out, row_out]
        out_shape = [jax.ShapeDtypeStruct((b, tp, D), BF16), jax.ShapeDtypeStruct((b, tp, D), F32)]
    else:
        v_first, v0, v1, v2 = vres
        in_specs += [row_out, full(v0), full(v1), full(v2)]
        args += [v_first, v0, v1, v2]
        out_specs = row_out
        out_shape = jax.ShapeDtypeStruct((b, tp, D), BF16)
    return pl.pallas_call(
        functools.partial(_rwkv_kernel, vres is not None),
        grid=(b, tp // tb),
        in_specs=in_specs,
        out_specs=out_specs,
        out_shape=out_shape,
        scratch_shapes=scratch,
        compiler_params=_cp("parallel", "arbitrary"),
        name="rwkv7",
    )(*args)


def _merge_kernel(pad, s_ref, ya_ref, yb_ref, yc_ref, yd_ref, g0_ref, g1_ref, g2_ref, g3_ref, wb_ref, wo_ref,
                  o_ref):
    acc = None
    for n, (y_ref, g_ref) in enumerate(((ya_ref, g0_ref), (yb_ref, g1_ref), (yc_ref, g2_ref), (yd_ref, g3_ref))):
        t = jax.nn.sigmoid(g_ref[...]) * jnp.dot(y_ref[...], wb_ref[n], preferred_element_type=F32)
        acc = t if acc is None else acc + t
    out = s_ref[...] + _dot(acc, wo_ref[...])
    o_ref[...] = jnp.where(_valid_rows(out.shape, pl.program_id(1), pad), out, 0.0)


def _merge(s, ys, p_all, pad, wb, wo, tm):
    b, tp, d = s.shape
    row = pl.BlockSpec((None, tm, d), lambda bi, i: (bi, i, 0))
    gate = lambda n: pl.BlockSpec((None, tm, d), lambda bi, i: (bi, i, COL_GATE // D + n))
    return pl.pallas_call(
        functools.partial(_merge_kernel, pad),
        grid=(b, tp // tm),
        in_specs=[row, row, row, row, row, gate(0), gate(1), gate(2), gate(3),
                  pl.BlockSpec(wb.shape, lambda bi, i: (0, 0, 0)), pl.BlockSpec(wo.shape, lambda bi, i: (0, 0))],
        out_specs=row,
        out_shape=jax.ShapeDtypeStruct(s.shape, F32),
        compiler_params=_cp("parallel", "parallel"),
        name="merge",
    )(s, *ys, p_all, p_all, p_all, p_all, wb, wo)


def _ffn_kernel(pad, s_ref, nw_ref, wg_ref, wu_ref, wo_ref, o_ref, hn_ref, acc_ref):
    row_tile = pl.program_id(1)
    h = pl.program_id(2)
    last_h = pl.num_programs(2) - 1

    @pl.when(h == 0)
    def _():
        _rms_norm_to(hn_ref, s_ref, nw_ref)
        acc_ref[...] = jnp.zeros_like(acc_ref)

    hn = hn_ref[...]
    g = jnp.dot(hn, wg_ref[...], preferred_element_type=F32)
    u = jnp.dot(hn, wu_ref[...], preferred_element_type=F32)
    acc_ref[...] += _dot(_silu(g) * u, wo_ref[...])

    @pl.when(h == last_h)
    def _():
        out = s_ref[...] + acc_ref[...]
        o_ref[...] = jnp.where(_valid_rows(out.shape, row_tile, pad), out, 0.0)


def _ffn(s, pad, nw, w_in, w_out, tm, th):
    b, tp, d = s.shape
    hidden = w_out.shape[0]
    nh = hidden // th
    row = pl.BlockSpec((None, tm, d), lambda bi, i, h: (bi, i, 0))
    return pl.pallas_call(
        functools.partial(_ffn_kernel, pad),
        grid=(b, tp // tm, nh),
        in_specs=[row, pl.BlockSpec((1, d), lambda bi, i, h: (0, 0)),
                  pl.BlockSpec((d, th), lambda bi, i, h: (0, h)),
                  pl.BlockSpec((d, th), lambda bi, i, h: (0, nh + h)),
                  pl.BlockSpec((th, d), lambda bi, i, h: (h, 0))],
        out_specs=row,
        out_shape=jax.ShapeDtypeStruct(s.shape, F32),
        scratch_shapes=[pltpu.VMEM((tm, d), BF16), pltpu.VMEM((tm, d), F32)],
        compiler_params=_cp("parallel", "parallel", "arbitrary"),
        name="ffn",
    )(s, nw, w_in, w_in, w_out)


def _final_norm_kernel(x_ref, w_ref, o_ref):
    x = x_ref[...]
    o_ref[...] = x * lax.rsqrt(jnp.mean(x * x, axis=-1, keepdims=True) + NORM_EPS) * w_ref[...]


def _final_norm(s, w, tm):
    b, tp, d = s.shape
    row = pl.BlockSpec((None, tm, d), lambda bi, i: (bi, i, 0))
    return pl.pallas_call(
        _final_norm_kernel,
        grid=(b, tp // tm),
        in_specs=[row, pl.BlockSpec((1, d), lambda bi, i: (0, 0))],
        out_specs=row,
        out_shape=jax.ShapeDtypeStruct(s.shape, F32),
        compiler_params=_cp("parallel", "parallel"),
        name="final_norm",
    )(s, w)


def _pad_rows(a, rows, front=0):
    return jnp.pad(a, ((front, rows - front - a.shape[0]), (0, 0)))


def _in_weight(w):
    sizes = [D, D, D, 64, 64, 128, D, 2 * D, SSM_HEADS, D // 2, D // 2, D, D, D, D, 4 * D]
    r, k, v, wd, ad, gd, z, xbc, dt, q, kq, rv, rg, ly, lx, gates = jnp.split(w, np.cumsum(sizes)[:-1].tolist(), axis=1)
    cols = [gates, ly, lx, xbc, z, q, kq, rv, rg, r, k, v, wd, ad, gd, dt]
    out = jnp.concatenate(cols, axis=1)
    return jnp.pad(out, ((0, 0), (0, IN_COLS - out.shape[1]))).astype(BF16)


def _rope_tables(tp, pad):
    half = RET_DK // 2
    freq = jnp.power(ROPE_BASE, -jnp.arange(half, dtype=F32) / half)
    pos = jnp.arange(tp, dtype=F32) - pad
    ang = pos[:, None] * freq[None, :]
    cos = jnp.tile(jnp.concatenate([jnp.cos(ang), jnp.cos(ang)], axis=1), (1, 2 * RET_HEADS))
    sin = jnp.tile(jnp.concatenate([-jnp.sin(ang), jnp.sin(ang)], axis=1), (1, 2 * RET_HEADS))
    return cos, sin


def kernel(x, meta, norm_mix, norm_ffn, w_in, rwkv_mu, rwkv_w2, rwkv_a2, rwkv_g2, rwkv_vec, rwkv_rk, rwkv_v0,
           rwkv_v1, rwkv_v2, ssm_conv_w, ssm_conv_b, ssm_dt_bias, ssm_a_log, ssm_d, ssm_norm, lru_conv_w,
           lru_conv_b, lru_w_gate, lru_b_gate, lru_lambda, w_branch, w_out, w_ffn_in, w_ffn_out, final_norm):
    bsz, seq, d = x.shape
    depth = w_in.shape[0]
    t_len = seq + N_META
    pad = (-t_len) % CHUNK
    tp = t_len + pad
    s = jnp.concatenate([jnp.zeros((bsz, pad, d), x.dtype),
                         jnp.broadcast_to(meta[None].astype(x.dtype), (bsz, N_META, d)), x], axis=1)
    cos_t, sin_t = _rope_tables(tp, pad)
    tm_big = _row_tile(tp, 2112)
    tm_mid = _row_tile(tp, 528)
    tm_small = _row_tile(tp, 264)
    row = lambda a: a.reshape(1, -1)

    v_first = None
    for l in range(depth):
        p_all = _inproj(s, row(norm_mix[l]), _in_weight(w_in[l]), tm_big, 512)

        mu = rwkv_mu[l]
        vres = None
        if l > 0:
            vres = (v_first, row(rwkv_v0[l - 1]),
                    jnp.pad(rwkv_v1[l - 1], ((0, 0), (0, LANES - rwkv_v1.shape[-1]))).astype(BF16),
                    _pad_rows(rwkv_v2[l - 1], LANES).astype(BF16))
        res = _rwkv_mix(p_all, mu[:3 * D].reshape(3, D), row(mu[3 * D:]), rwkv_vec[l], row(rwkv_rk[l]),
                        _pad_rows(rwkv_w2[l], LANES).astype(BF16),
                        _pad_rows(rwkv_a2[l], LANES, front=64).astype(BF16), rwkv_g2[l].astype(BF16), vres)
        if l == 0:
            y_a, v_first = res
        else:
            y_a = res

        lanes16 = lambda a: jnp.pad(row(a), ((0, 0), (0, LANES - a.shape[0])))
        y_b = _ssd_mix(p_all, pad, ssm_conv_w[l], row(ssm_conv_b[l]), lanes16(ssm_dt_bias[l]),
                       lanes16(ssm_a_log[l]), row(jnp.repeat(ssm_d[l], HEAD)), row(ssm_norm[l]))
        y_c = _ret_mix(p_all, cos_t, sin_t)
        y_d = _lru_mix(p_all, pad, lru_conv_w[l], row(lru_conv_b[l]), lru_w_gate[l].astype(BF16), lru_b_gate[l],
                       row(lru_lambda[l]))

        s = _merge(s, (y_a, y_b, y_c, y_d), p_all, pad, w_branch[l].astype(BF16), w_out[l].astype(BF16), tm_small)
        s = _ffn(s, pad, row(norm_ffn[l]), w_ffn_in[l].astype(BF16), w_ffn_out[l].astype(BF16), tm_mid, 1408)
    return _final_norm(s, row(final_norm), tm_mid)[:, pad + N_META:]
```

```python
import functools
import math

import jax
import jax.numpy as jnp
from jax import lax
from jax.experimental import pallas as pl
from jax.experimental.pallas import tpu as pltpu

D = 1024
N_META = 16
NORM_EPS = 1e-6
CONV_K = 4
N_BRANCH = 4
CHUNK = 128
RWKV_CHUNK = 64
HEAD = 64
RWKV_LN_EPS = 64e-5
SSM_GROUPS = 4
SSM_STATE = 128
SSM_HEADS = 16
RET_HEADS = 8
RET_DK = 64
RET_DV = 128
ROPE_BASE = 10000.0
LRU_BLOCKS = 8
LRU_C = 8.0
FFN_HIDDEN = 2816
LANES = 128
SUBLANES = 8
VMEM_LIMIT = 56 * 1024 * 1024

COL_RWKV = 0
COL_Z = 3072
COL_XBC = 4096
COL_RET_QK = 6144
COL_RET_V = 7168
COL_RET_G = 8192
COL_LRU = 9216
COL_GATE = 11264
COL_RWKV_LR = 15360
COL_DT = 15616
IN_COLS = 15872

BF16 = jnp.bfloat16
F32 = jnp.float32


def _row_tile(tp, target, multiple=SUBLANES):
    best = multiple
    for t in range(multiple, target + 1, multiple):
        if tp % t == 0:
            best = t
    return best


def _cp(*sem):
    return pltpu.CompilerParams(dimension_semantics=sem, vmem_limit_bytes=VMEM_LIMIT)


def _iota(shape, dim):
    return lax.broadcasted_iota(jnp.int32, shape, dim)


def _dot(a, b):
    return jnp.dot(a.astype(BF16), b.astype(BF16), preferred_element_type=F32)


def _dot_nt(a, b):
    return lax.dot_general(a.astype(BF16), b.astype(BF16), (((1,), (1,)), ((), ())), preferred_element_type=F32)


def _dot_tn(a, b):
    return lax.dot_general(a.astype(BF16), b.astype(BF16), (((0,), (0,)), ((), ())), preferred_element_type=F32)


def _split_bf16(x):
    hi = x.astype(BF16)
    return hi, (x - hi.astype(F32)).astype(BF16)


def _expand_heads(xs, expand):
    his, los = zip(*[_split_bf16(x) for x in xs])
    out = jnp.dot(jnp.concatenate(his + los, axis=0), expand, preferred_element_type=F32)
    half = out.shape[0] // 2
    res, row = [], 0
    for x in xs:
        res.append(out[row:row + x.shape[0]] + out[half + row:half + row + x.shape[0]])
        row += x.shape[0]
    return res


def _softplus(x):
    return jnp.maximum(x, 0.0) + jnp.log1p(jnp.exp(-jnp.abs(x)))


def _silu(x):
    return x * jax.nn.sigmoid(x)


def _shift_rows(x, d, tail):
    xs = pltpu.roll(x, d, 0)
    ts = pltpu.roll(tail, d, 0)
    head = jnp.where(_iota(ts.shape, 0) < d, ts, xs[:SUBLANES])
    return jnp.concatenate([head, xs[SUBLANES:]], axis=0)


def _causal_conv(x, tail, w_ref, b_ref):
    acc = b_ref[...] + x * w_ref[CONV_K - 1:CONV_K, :]
    for d in range(1, CONV_K):
        acc = acc + _shift_rows(x, d, tail) * w_ref[CONV_K - 1 - d:CONV_K - d, :]
    return acc


def _scan_rows(a, u, carry):
    row = _iota((SUBLANES, u.shape[1]), 0)
    out = []
    for g in range(u.shape[0] // SUBLANES):
        rows = slice(g * SUBLANES, (g + 1) * SUBLANES)
        ug = u[rows]
        ag = None if a is None else a[rows]
        d = 1
        while d < SUBLANES:
            keep = row >= d
            u_prev = jnp.where(keep, pltpu.roll(ug, d, 0), 0.0)
            if ag is None:
                ug = ug + u_prev
            else:
                ug = ag * u_prev + ug
                ag = ag * jnp.where(keep, pltpu.roll(ag, d, 0), 1.0)
            d *= 2
        hg = ug + carry if ag is None else ug + ag * carry
        carry = hg[SUBLANES - 1:SUBLANES, :]
        out.append(hg)
    return jnp.concatenate(out, axis=0)


def _cumsum_rows(x):
    return _scan_rows(None, x, jnp.zeros((1, x.shape[1]), F32))


class _Rows:
    def __init__(self, ref, rows, n_rows):
        self.ref, self.rows = ref, rows
        self.shape = (n_rows, ref.shape[1])
        self.dtype = ref.dtype

    def __getitem__(self, idx):
        assert idx is Ellipsis
        return self.ref[self.rows, :]

    def __setitem__(self, idx, value):
        assert idx is Ellipsis
        self.ref[self.rows, :] = value


def _chunk_loop(chunk_fn, sub, n_in, n_param, n_out, *refs):
    ins, params = refs[:n_in], refs[n_in:n_in + n_param]
    outs, scratch = refs[n_in + n_param:n_in + n_param + n_out], refs[n_in + n_param + n_out:]
    block = pl.program_id(1)

    @pl.when(block == 0)
    def _():
        for r in scratch:
            r[...] = jnp.zeros_like(r)

    n = ins[0].shape[0] // sub

    def body(i, carry):
        rows = pl.ds(pl.multiple_of(i * sub, sub), sub)
        view = lambda r: _Rows(r, rows, sub)
        chunk_fn(block * n + i, *map(view, ins), *params, *map(view, outs), *scratch)
        return carry

    lax.fori_loop(0, n, body, 0)


def _mixer_block(tp, sub):
    return 3 * sub if tp % (3 * sub) == 0 else sub


def _valid_rows(shape, chunk_idx, pad):
    return _iota(shape, 0) + chunk_idx * shape[0] >= pad


def _rms_norm_to(hn_ref, x_ref, nw_ref):
    tm = x_ref.shape[0]
    sub = _row_tile(tm, 512, 2 * SUBLANES)

    def body(i, carry):
        rows = pl.ds(pl.multiple_of(i * sub, 2 * SUBLANES), sub)
        x = x_ref[rows, :]
        y = x * lax.rsqrt(jnp.mean(x * x, axis=-1, keepdims=True) + NORM_EPS)
        hn_ref[rows, :] = (y * nw_ref[...]).astype(BF16)
        return carry

    lax.fori_loop(0, tm // sub, body, 0)


def _inproj_kernel(x_ref, w_ref, o_ref):
    o_ref[...] = jnp.dot(x_ref[...], w_ref[...], preferred_element_type=F32)


def _inproj(hn, w, layer, tm, tn):
    b, tp, d = hn.shape
    n = w.shape[2]
    return pl.pallas_call(
        _inproj_kernel,
        grid=(b, tp // tm, n // tn),
        in_specs=[
            pl.BlockSpec((None, tm, d), lambda bi, i, j: (bi, i, 0)),
            pl.BlockSpec((None, d, tn), lambda bi, i, j: (layer, 0, j)),
        ],
        out_specs=pl.BlockSpec((None, tm, tn), lambda bi, i, j: (bi, i, j)),
        out_shape=jax.ShapeDtypeStruct((b, tp, n), F32),
        compiler_params=_cp("parallel", "parallel", "arbitrary"),
        name="inproj",
    )(hn, w)


def _lru_chunk(pad, c, y_ref, x_ref, cw_ref, cb_ref, wg_ref, bg_ref, lam_ref, o_ref, tail_ref, h_ref):
    x = x_ref[...]
    n_rows = x.shape[0]
    xc = _causal_conv(x, tail_ref[...], cw_ref, cb_ref)
    tail_ref[...] = x[n_rows - SUBLANES:]
    xcb = xc.astype(BF16)
    blk = D // LRU_BLOCKS

    def gates(k):
        parts = [jnp.dot(xcb[:, n * blk:(n + 1) * blk], wg_ref[k, n], preferred_element_type=F32)
                 for n in range(LRU_BLOCKS)]
        return jax.nn.sigmoid(jnp.concatenate(parts, axis=1) + bg_ref[k:k + 1, :])

    r_gate = gates(0)
    i_gate = gates(1)
    log_a = -LRU_C * r_gate * _softplus(-lam_ref[...])
    a = jnp.exp(log_a)
    u = jnp.sqrt(1.0 - jnp.exp(2.0 * log_a)) * (i_gate * xc)
    u = jnp.where(_valid_rows(u.shape, c, pad), u, 0.0)
    h = _scan_rows(a, u, h_ref[SUBLANES - 1:SUBLANES, :])
    h_ref[...] = h[n_rows - SUBLANES:]
    o_ref[...] = (h * jax.nn.gelu(y_ref[...])).astype(o_ref.dtype)


def _lru_mix(p_all, pad, cw, cb, wg, bg, lam):
    b, tp, _ = p_all.shape
    blk = _mixer_block(tp, CHUNK)
    full = lambda a: pl.BlockSpec(a.shape, lambda bi, c: (0,) * a.ndim)
    return pl.pallas_call(
        functools.partial(_chunk_loop, functools.partial(_lru_chunk, pad), CHUNK, 2, 5, 1),
        grid=(b, tp // blk),
        in_specs=[
            pl.BlockSpec((None, blk, D), lambda bi, c: (bi, c, COL_LRU // D)),
            pl.BlockSpec((None, blk, D), lambda bi, c: (bi, c, COL_LRU // D + 1)),
            full(cw), full(cb), full(wg), full(bg), full(lam),
        ],
        out_specs=pl.BlockSpec((None, blk, D), lambda bi, c: (bi, c, 0)),
        out_shape=jax.ShapeDtypeStruct((b, tp, D), BF16),
        scratch_shapes=[pltpu.VMEM((SUBLANES, D), F32), pltpu.VMEM((SUBLANES, D), F32)],
        compiler_params=_cp("parallel", "arbitrary"),
        name="rglru",
    )(p_all, p_all, cw, cb, wg, bg, lam)


def _ssd_chunk(pad, c, xbc_ref, z_ref, dt_ref, cw_ref, cb_ref, dtb_ref, alog_ref, dskip_ref, nw_ref,
               o_ref, tail_ref, state_ref):
    raw = xbc_ref[...]
    n_rows = raw.shape[0]
    xbc = _silu(_causal_conv(raw, tail_ref[...], cw_ref, cb_ref))
    tail_ref[...] = raw[n_rows - SUBLANES:]
    gn = SSM_GROUPS * SSM_STATE
    xs = jnp.where(_valid_rows((n_rows, D), c, pad), xbc[:, :D], 0.0)
    bm = xbc[:, D:D + gn]
    cm = xbc[:, D + gn:]

    dt = _softplus(dt_ref[...] + dtb_ref[...])
    a_cum = _cumsum_rows(dt * -jnp.exp(alog_ref[...]))
    a_last = a_cum[n_rows - 1:n_rows, :]
    expand = (_iota((LANES, D), 1) // HEAD == _iota((LANES, D), 0)).astype(BF16)
    dt_x, ecum_x, eend_x, elast_x = _expand_heads(
        [dt, jnp.exp(a_cum), jnp.exp(a_last - a_cum), jnp.broadcast_to(jnp.exp(a_last), (2 * SUBLANES, LANES))],
        expand)
    xdt = xs * dt_x
    xdt_end = xdt * eend_x
    a_cum_t = a_cum.T

    tril = _iota((n_rows, n_rows), 0) >= _iota((n_rows, n_rows), 1)
    lane_lo = _iota((n_rows, LANES), 1) < HEAD
    heads_per_group = SSM_HEADS // SSM_GROUPS
    gw = heads_per_group * HEAD
    outs = []
    for g in range(SSM_GROUPS):
        b_g = bm[:, g * SSM_STATE:(g + 1) * SSM_STATE]
        c_g = cm[:, g * SSM_STATE:(g + 1) * SSM_STATE]
        cb = _dot_nt(c_g, b_g)
        s_prev = state_ref[g]
        y_off = _dot(c_g, s_prev) * ecum_x[:, g * gw:(g + 1) * gw]
        state_ref[g] = s_prev * elast_x[:1, g * gw:(g + 1) * gw] + _dot_tn(b_g, xdt_end[:, g * gw:(g + 1) * gw])
        y_diag = []
        for pr in range(heads_per_group // 2):
            col = g * gw + pr * LANES
            x_pair = xdt[:, col:col + LANES]
            acc = None
            for e in range(2):
                h = (col // HEAD) + e
                seg = jnp.where(tril, a_cum[:, h:h + 1] - a_cum_t[h:h + 1, :], -1e30)
                m = cb * jnp.exp(seg)
                x_h = jnp.where(lane_lo if e == 0 else ~lane_lo, x_pair, 0.0)
                t = _dot(m, x_h)
                acc = t if acc is None else acc + t
            y_diag.append(acc)
        outs.append(jnp.concatenate(y_diag, axis=1) + y_off)
    y = jnp.concatenate(outs, axis=1) + xs * dskip_ref[...]
    y = y * _silu(z_ref[...])
    normed = []
    for g in range(SSM_GROUPS):
        yg = y[:, g * gw:(g + 1) * gw]
        normed.append(yg * lax.rsqrt(jnp.mean(yg * yg, axis=-1, keepdims=True) + 1e-5))
    o_ref[...] = (jnp.concatenate(normed, axis=1) * nw_ref[...]).astype(o_ref.dtype)


def _ssd_mix(p_all, pad, cw, cb, dtb, alog, dskip, nw):
    b, tp, _ = p_all.shape
    blk = _mixer_block(tp, CHUNK)
    full = lambda a: pl.BlockSpec(a.shape, lambda bi, c: (0,) * a.ndim)
    return pl.pallas_call(
        functools.partial(_chunk_loop, functools.partial(_ssd_chunk, pad), CHUNK, 3, 6, 1),
        grid=(b, tp // blk),
        in_specs=[
            pl.BlockSpec((None, blk, 2 * D), lambda bi, c: (bi, c, COL_XBC // (2 * D))),
            pl.BlockSpec((None, blk, D), lambda bi, c: (bi, c, COL_Z // D)),
            pl.BlockSpec((None, blk, LANES), lambda bi, c: (bi, c, COL_DT // LANES)),
            full(cw), full(cb), full(dtb), full(alog), full(dskip), full(nw),
        ],
        out_specs=pl.BlockSpec((None, blk, D), lambda bi, c: (bi, c, 0)),
        out_shape=jax.ShapeDtypeStruct((b, tp, D), BF16),
        scratch_shapes=[pltpu.VMEM((SUBLANES, 2 * D), F32),
                        pltpu.VMEM((SSM_GROUPS, SSM_STATE, D // SSM_GROUPS), F32)],
        compiler_params=_cp("parallel", "arbitrary"),
        name="ssd",
    )(p_all, p_all, p_all, cw, cb, dtb, alog, dskip, nw)


_RET_LOG_G = [math.log1p(-2.0 ** (-5.0 - h)) for h in range(RET_HEADS)]


def _ret_chunk(c, qk_ref, v_ref, g_ref, cos_ref, sin_ref, o_ref, state_ref):
    qk = qk_ref[...]
    n_rows, width = qk.shape
    half = RET_DK // 2
    first = (_iota(qk.shape, 1) % RET_DK) < half
    partner = jnp.where(first, pltpu.roll(qk, width - half, 1), pltpu.roll(qk, half, 1))
    qk = qk * cos_ref[...] + partner * sin_ref[...]
    hw = RET_HEADS * RET_DK
    q = qk[:, :hw]
    k = qk[:, hw:] * (RET_DK ** -0.5)
    v = v_ref[...]

    rowf = _iota((n_rows, LANES), 0).astype(F32)
    rel = (_iota((n_rows, n_rows), 0) - _iota((n_rows, n_rows), 1)).astype(F32)
    lane_lo = _iota((n_rows, LANES), 1) < RET_DK
    outs = []
    for h in range(RET_HEADS):
        log_g = _RET_LOG_G[h]
        col = (h // 2) * LANES
        mask = lane_lo if h % 2 == 0 else ~lane_lo
        q_h = jnp.where(mask, q[:, col:col + LANES], 0.0)
        k_h = jnp.where(mask, k[:, col:col + LANES], 0.0)
        v_h = v[:, h * RET_DV:(h + 1) * RET_DV]
        intra = jnp.where(rel >= 0, jnp.exp(jnp.maximum(rel, 0.0) * log_g), 0.0)
        y_in = _dot(_dot_nt(q_h, k_h) * intra, v_h)
        r_prev = state_ref[h]
        y_x = _dot(q_h, r_prev) * jnp.exp((rowf + 1.0) * log_g)
        k_dec = k_h * jnp.exp((n_rows - 1.0 - rowf) * log_g)
        state_ref[h] = math.exp(n_rows * log_g) * r_prev + _dot_tn(k_dec, v_h)
        y = y_in + y_x
        outs.append(y * lax.rsqrt(jnp.mean(y * y, axis=-1, keepdims=True) + NORM_EPS))
    o_ref[...] = (_silu(g_ref[...]) * jnp.concatenate(outs, axis=1)).astype(o_ref.dtype)


def _ret_mix(p_all, cos_t, sin_t):
    b, tp, _ = p_all.shape
    blk = _mixer_block(tp, CHUNK)
    return pl.pallas_call(
        functools.partial(_chunk_loop, _ret_chunk, CHUNK, 5, 0, 1),
        grid=(b, tp // blk),
        in_specs=[
            pl.BlockSpec((None, blk, D), lambda bi, c: (bi, c, COL_RET_QK // D)),
            pl.BlockSpec((None, blk, D), lambda bi, c: (bi, c, COL_RET_V // D)),
            pl.BlockSpec((None, blk, D), lambda bi, c: (bi, c, COL_RET_G // D)),
            pl.BlockSpec((blk, D), lambda bi, c: (c, 0)),
            pl.BlockSpec((blk, D), lambda bi, c: (c, 0)),
        ],
        out_specs=pl.BlockSpec((None, blk, D), lambda bi, c: (bi, c, 0)),
        out_shape=jax.ShapeDtypeStruct((b, tp, D), BF16),
        scratch_shapes=[pltpu.VMEM((RET_HEADS, LANES, RET_DV), F32)],
        compiler_params=_cp("parallel", "arbitrary"),
        name="retention",
    )(p_all, p_all, p_all, cos_t, sin_t)


def _head_sums(xs, ones_bd):
    n_rows, width = xs[0].shape
    tiles = width // LANES
    parts = []
    for x in xs:
        for t in _split_bf16(x):
            parts += [t[:, j * LANES:(j + 1) * LANES] for j in range(tiles)]
    sums = jnp.dot(jnp.concatenate(parts, axis=0), ones_bd, preferred_element_type=F32)
    out = []
    for i in range(len(xs)):
        base = i * 2 * tiles
        cols = [sums[(base + j) * n_rows:(base + j + 1) * n_rows] +
                sums[(base + tiles + j) * n_rows:(base + tiles + j + 1) * n_rows] for j in range(tiles)]
        out.append(jnp.concatenate(cols, axis=1))
    return out


def _rwkv_chunk(has_vres, c, *refs):
    del c
    if has_vres:
        (r_ref, k_ref, v_ref, lr_ref, vf_ref, mu_ref, mulr_ref, vec_ref, rk_ref, w2_ref, a2_ref, g2_ref,
         v0_ref, v1_ref, v2_ref, o_ref, tail_ref, taillr_ref, state_ref) = refs
    else:
        (r_ref, k_ref, v_ref, lr_ref, mu_ref, mulr_ref, vec_ref, rk_ref, w2_ref, a2_ref, g2_ref,
         o_ref, vown_ref, tail_ref, taillr_ref, state_ref) = refs
    n_rows = r_ref.shape[0]

    def mixed(ref, tail, mu):
        x = ref[...]
        return x + (_shift_rows(x, 1, tail) - x) * mu, x[n_rows - SUBLANES:]

    r, t0 = mixed(r_ref, tail_ref[0], mu_ref[0:1, :])
    k, t1 = mixed(k_ref, tail_ref[1], mu_ref[1:2, :])
    v, t2 = mixed(v_ref, tail_ref[2], mu_ref[2:3, :])
    lr, t3 = mixed(lr_ref, taillr_ref[...], mulr_ref[...])
    tail_ref[0] = t0
    tail_ref[1] = t1
    tail_ref[2] = t2
    taillr_ref[...] = t3

    w0, a0, k_k, k_a, ln_w, ln_b = (vec_ref[i:i + 1, :] for i in range(6))
    wa = lr[:, :LANES]
    w_log = -_softplus(-(w0 + _dot(jnp.tanh(wa), w2_ref[...]))) - 0.5
    log_decay = -jnp.exp(w_log)
    a_sig = jax.nn.sigmoid(a0 + _dot(wa, a2_ref[...]))
    gate = _dot(jax.nn.sigmoid(lr[:, LANES:]), g2_ref[...])
    if has_vres:
        mixv = jax.nn.sigmoid(v0_ref[...] + _dot(_dot(v, v1_ref[...]), v2_ref[...]))
        v = v + (vf_ref[...] - v) * mixv
    else:
        vown_ref[...] = v

    ones_bd = (_iota((LANES, LANES), 0) // HEAD == _iota((LANES, LANES), 1) // HEAD).astype(BF16)
    kk = k * k_k
    k = k * (1.0 + (a_sig - 1.0) * k_a)
    kk_sq, bonus_rk = _head_sums([kk * kk, r * k * rk_ref[...]], ones_bd)
    kk = kk / jnp.maximum(jnp.sqrt(kk_sq), 1e-12)
    a_vec = -kk
    b_vec = kk * a_sig

    cum = _cumsum_rows(log_decay)
    cum_last = cum[n_rows - 1:n_rows, :]
    g_inc = jnp.exp(cum)
    g_inv = jnp.exp(-cum)
    to_end = jnp.exp(cum_last - cum)
    r_t = r * g_inc
    k_t = k * g_inv
    b_t = b_vec * g_inv
    a_t = a_vec * jnp.exp(cum - log_decay)
    k_end = k * to_end
    b_end = b_vec * to_end
    decay_all = jnp.broadcast_to(jnp.exp(cum_last), (LANES, D))

    n4 = 4 * n_rows
    row4 = _iota((n4, 2 * n_rows), 0)
    col4 = _iota((n4, 2 * n_rows), 1)
    score_mask = (row4 % n_rows) + jnp.where(row4 >= 2 * n_rows, 1, 0) > (col4 % n_rows)
    lane_lo = _iota((n_rows, LANES), 1) < HEAD
    eye = (_iota((LANES, LANES), 0) == _iota((LANES, LANES), 1)).astype(F32)
    blockdiag = (_iota((LANES, LANES), 0) // HEAD) == (_iota((LANES, LANES), 1) // HEAD)
    zeros_half = jnp.zeros((n_rows, LANES), F32)

    pairs = range(D // LANES)
    sls = [slice(j * LANES, (j + 1) * LANES) for j in pairs]
    ms, invs, powers = [], [], []
    for sl in sls:
        at, rt = a_t[:, sl], r_t[:, sl]
        lhs = jnp.concatenate([jnp.where(lane_lo, at, 0.0), jnp.where(lane_lo, 0.0, at),
                               jnp.where(lane_lo, rt, 0.0), jnp.where(lane_lo, 0.0, rt)], axis=0)
        m = jnp.where(score_mask, _dot_nt(lhs, jnp.concatenate([k_t[:, sl], b_t[:, sl]], axis=0)), 0.0)
        ms.append(m)
    for m in ms:
        a_ab = jnp.where(blockdiag,
                         jnp.concatenate([pltpu.roll(m[:n_rows], HEAD, 1), m[n_rows:2 * n_rows]], axis=0), 0.0)
        invs.append(eye + a_ab)
        powers.append(a_ab)
    span = 1
    while 2 * span < n_rows:
        powers = [_dot(p, p) for p in powers]
        invs = [x + _dot(x, p) for x, p in zip(invs, powers)]
        span *= 2
    s_prevs = [state_ref[j] for j in pairs]
    projs = [_dot(jnp.concatenate([a_t[:, sl], r_t[:, sl]], axis=0), s) for sl, s in zip(sls, s_prevs)]
    in_k = _iota((2 * n_rows, 2 * n_rows), 1) < n_rows
    mvs = [_dot(jnp.where(in_k, m[:2 * n_rows], 0.0), jnp.concatenate([v[:, sl], zeros_half], axis=0))
           for sl, m in zip(sls, ms)]
    u_sts = [_dot(x, jnp.where(blockdiag, jnp.concatenate([pj[:n_rows], pj[:n_rows]], axis=0) + mv, 0.0))
             for x, pj, mv in zip(invs, projs, mvs)]
    ys = []
    for j, sl, m, pj, u_st, s_prev in zip(pairs, sls, ms, projs, u_sts, s_prevs):
        vp = v[:, sl]
        u0, u1 = u_st[:n_rows], u_st[n_rows:]
        vu = jnp.concatenate([jnp.where(lane_lo, vp, 0.0), u0, jnp.where(lane_lo, 0.0, vp), u1], axis=0)
        m_r = jnp.concatenate([m[2 * n_rows:3 * n_rows], m[3 * n_rows:]], axis=1)
        ys.append(pj[n_rows:] + _dot(m_r, vu))
        upd = _dot_tn(jnp.concatenate([k_end[:, sl], b_end[:, sl]], axis=0),
                      jnp.concatenate([vp, u0 + u1], axis=0))
        state_ref[j] = decay_all[:, sl].T * s_prev + jnp.where(blockdiag, upd, 0.0)
    y = jnp.concatenate(ys, axis=1)

    yc = y - _head_sums([y], ones_bd)[0] * (1.0 / HEAD)
    var = _head_sums([yc * yc], ones_bd)[0] * (1.0 / HEAD)
    y = yc * lax.rsqrt(var + RWKV_LN_EPS) * ln_w + ln_b
    o_ref[...] = ((y + bonus_rk * v) * gate).astype(o_ref.dtype)


def _rwkv_mix(p_all, mu, mu_lr, vec, rk, w2, a2, g2, vres):
    b, tp, _ = p_all.shape
    tb = _mixer_block(tp, RWKV_CHUNK)
    full = lambda a: pl.BlockSpec(a.shape, lambda bi, c: (0,) * a.ndim)
    col = lambda off, w: pl.BlockSpec((None, tb, w), lambda bi, c: (bi, c, off // w))
    row_out = pl.BlockSpec((None, tb, D), lambda bi, c: (bi, c, 0))
    blocks = [col(COL_RWKV, D), col(COL_RWKV + D, D), col(COL_RWKV + 2 * D, D), col(COL_RWKV_LR, 2 * LANES)]
    block_args = [p_all, p_all, p_all, p_all]
    params = [mu, mu_lr, vec, rk, w2, a2, g2]
    scratch = [pltpu.VMEM((3, SUBLANES, D), F32), pltpu.VMEM((SUBLANES, 2 * LANES), F32),
               pltpu.VMEM((D // LANES, LANES, LANES), F32)]
    if vres is None:
        out_specs = [row_out, row_out]
        out_shape = [jax.ShapeDtypeStruct((b, tp, D), BF16), jax.ShapeDtypeStruct((b, tp, D), F32)]
    else:
        v_first, v0, v1, v2 = vres
        blocks.append(row_out)
        block_args.append(v_first)
        params += [v0, v1, v2]
        out_specs = [row_out]
        out_shape = [jax.ShapeDtypeStruct((b, tp, D), BF16)]
    in_specs = blocks + [full(a) for a in params]
    args = block_args + params
    chunk_fn = functools.partial(_rwkv_chunk, vres is not None)
    return pl.pallas_call(
        functools.partial(_chunk_loop, chunk_fn, RWKV_CHUNK, len(blocks), len(params), len(out_shape)),
        grid=(b, tp // tb),
        in_specs=in_specs,
        out_specs=out_specs,
        out_shape=out_shape,
        scratch_shapes=scratch,
        compiler_params=_cp("parallel", "arbitrary"),
        name="rwkv7",
    )(*args)


def _merge_kernel(pad, s_ref, ya_ref, yb_ref, yc_ref, yd_ref, g0_ref, g1_ref, g2_ref, g3_ref, wb_ref, wo_ref,
                  o_ref):
    acc = None
    for n, (y_ref, g_ref) in enumerate(((ya_ref, g0_ref), (yb_ref, g1_ref), (yc_ref, g2_ref), (yd_ref, g3_ref))):
        t = jax.nn.sigmoid(g_ref[...]) * jnp.dot(y_ref[...], wb_ref[n], preferred_element_type=F32)
        acc = t if acc is None else acc + t
    out = s_ref[...] + _dot(acc, wo_ref[...])
    o_ref[...] = jnp.where(_valid_rows(out.shape, pl.program_id(1), pad), out, 0.0)


def _merge(s, ys, p_all, pad, wb, wo, layer, tm):
    b, tp, d = s.shape
    row = pl.BlockSpec((None, tm, d), lambda bi, i: (bi, i, 0))
    gate = lambda n: pl.BlockSpec((None, tm, d), lambda bi, i: (bi, i, COL_GATE // D + n))
    return pl.pallas_call(
        functools.partial(_merge_kernel, pad),
        grid=(b, tp // tm),
        in_specs=[row, row, row, row, row, gate(0), gate(1), gate(2), gate(3),
                  pl.BlockSpec((None,) + wb.shape[1:], lambda bi, i: (layer, 0, 0, 0), pipeline_mode=pl.Buffered(1)),
                  pl.BlockSpec((None,) + wo.shape[1:], lambda bi, i: (layer, 0, 0), pipeline_mode=pl.Buffered(1))],
        out_specs=row,
        out_shape=jax.ShapeDtypeStruct(s.shape, F32),
        compiler_params=_cp("parallel", "parallel"),
        name="merge",
    )(s, *ys, p_all, p_all, p_all, p_all, wb, wo)


FFN_COLS = 256


def _ffn_kernel(pad, last, s_ref, nw_ref, wi_ref, wo_ref, nxt_ref, *refs):
    if last:
        o_ref, hn_ref, acc_ref = refs
    else:
        o_ref, hn_out_ref, hn_ref, acc_ref = refs
    row_tile = pl.program_id(1)
    tm = s_ref.shape[0]
    hidden = wo_ref.shape[0]
    _rms_norm_to(hn_ref, s_ref, nw_ref)
    for c in range(hidden // FFN_COLS):
        cols = slice(c * FFN_COLS, (c + 1) * FFN_COLS)
        hn = hn_ref[...]
        g = jnp.dot(hn, wi_ref[:, cols], preferred_element_type=F32)
        u = jnp.dot(hn, wi_ref[:, hidden + c * FFN_COLS:hidden + (c + 1) * FFN_COLS], preferred_element_type=F32)
        t = _dot(_silu(g) * u, wo_ref[cols, :])
        if c == 0:
            acc_ref[...] = t
        else:
            acc_ref[...] += t

    sub = _row_tile(tm, 512, 2 * SUBLANES)

    def body(i, carry):
        rows = pl.ds(pl.multiple_of(i * sub, 2 * SUBLANES), sub)
        out = s_ref[rows, :] + acc_ref[rows, :]
        valid = _iota(out.shape, 0) + (row_tile * tm + i * sub) >= pad
        out = jnp.where(valid, out, 0.0)
        normed = out * lax.rsqrt(jnp.mean(out * out, axis=-1, keepdims=True) + NORM_EPS) * nxt_ref[...]
        if last:
            o_ref[rows, :] = normed
        else:
            o_ref[rows, :] = out
            hn_out_ref[rows, :] = normed.astype(BF16)
        return carry

    lax.fori_loop(0, tm // sub, body, 0)


def _ffn(s, pad, nw, w_in, w_out, layer, nxt_w, last, tm):
    b, tp, d = s.shape
    row = pl.BlockSpec((None, tm, d), lambda bi, i: (bi, i, 0))
    vec = pl.BlockSpec((1, d), lambda bi, i: (0, 0))
    resident = lambda w: pl.BlockSpec((None,) + w.shape[1:], lambda bi, i: (layer, 0, 0),
                                      pipeline_mode=pl.Buffered(1))
    out_specs, out_shape = row, jax.ShapeDtypeStruct(s.shape, F32)
    if not last:
        out_specs, out_shape = [row, row], [out_shape, jax.ShapeDtypeStruct(s.shape, BF16)]
    return pl.pallas_call(
        functools.partial(_ffn_kernel, pad, last),
        grid=(b, tp // tm),
        in_specs=[row, vec, resident(w_in), resident(w_out), vec],
        out_specs=out_specs,
        out_shape=out_shape,
        scratch_shapes=[pltpu.VMEM((tm, d), BF16), pltpu.VMEM((tm, d), F32)],
        compiler_params=_cp("parallel", "parallel"),
        name="ffn",
    )(s, nw, w_in, w_out, nxt_w)


def _norm_kernel(x_ref, w_ref, o_ref):
    _rms_norm_to(o_ref, x_ref, w_ref)


def _norm_bf16(s, w, tm):
    b, tp, d = s.shape
    row = pl.BlockSpec((None, tm, d), lambda bi, i: (bi, i, 0))
    return pl.pallas_call(
        _norm_kernel,
        grid=(b, tp // tm),
        in_specs=[row, pl.BlockSpec((1, d), lambda bi, i: (0, 0))],
        out_specs=row,
        out_shape=jax.ShapeDtypeStruct(s.shape, BF16),
        compiler_params=_cp("parallel", "parallel"),
        name="first_norm",
    )(s, w)


def _pad_rows(a, rows, front=0):
    return jnp.pad(a, ((front, rows - front - a.shape[0]), (0, 0)))


def _in_weight(w):
    lr0, lr1 = 3 * D, 3 * D + 2 * LANES
    dt0 = lr1 + 3 * D
    dt1 = dt0 + SSM_HEADS
    tail = jnp.zeros(w.shape[:-1] + (IN_COLS - w.shape[-1],), w.dtype)
    out = jnp.concatenate([w[..., :lr0], w[..., lr1:dt0], w[..., dt1:], w[..., lr0:lr1], w[..., dt0:dt1], tail],
                          axis=-1)
    return out.astype(BF16)


def _rope_tables(tp, pad):
    half = RET_DK // 2
    freq = jnp.power(ROPE_BASE, -jnp.arange(half, dtype=F32) / half)
    pos = jnp.arange(tp, dtype=F32) - pad
    ang = pos[:, None] * freq[None, :]
    cos = jnp.tile(jnp.concatenate([jnp.cos(ang), jnp.cos(ang)], axis=1), (1, 2 * RET_HEADS))
    sin = jnp.tile(jnp.concatenate([-jnp.sin(ang), jnp.sin(ang)], axis=1), (1, 2 * RET_HEADS))
    return cos, sin


def kernel(x, meta, norm_mix, norm_ffn, w_in, rwkv_mu, rwkv_w2, rwkv_a2, rwkv_g2, rwkv_vec, rwkv_rk, rwkv_v0,
           rwkv_v1, rwkv_v2, ssm_conv_w, ssm_conv_b, ssm_dt_bias, ssm_a_log, ssm_d, ssm_norm, lru_conv_w,
           lru_conv_b, lru_w_gate, lru_b_gate, lru_lambda, w_branch, w_out, w_ffn_in, w_ffn_out, final_norm):
    bsz, seq, d = x.shape
    depth = w_in.shape[0]
    t_len = seq + N_META
    pad = (-t_len) % CHUNK
    tp = t_len + pad
    s = jnp.concatenate([jnp.zeros((bsz, pad, d), x.dtype),
                         jnp.broadcast_to(meta[None].astype(x.dtype), (bsz, N_META, d)), x], axis=1)
    cos_t, sin_t = _rope_tables(tp, pad)
    tm_big = _row_tile(tp, 1056, 2 * SUBLANES)
    tm_mid = _row_tile(tp, 528, 2 * SUBLANES)
    row = lambda a: a.reshape(1, -1)

    w_in_b = _in_weight(w_in)
    w_branch_b, w_out_b = w_branch.astype(BF16), w_out.astype(BF16)
    w_ffn_in_b, w_ffn_out_b = w_ffn_in.astype(BF16), w_ffn_out.astype(BF16)

    v_first = None
    hn = _norm_bf16(s, row(norm_mix[0]), tm_mid)
    for l in range(depth):
        p_all = _inproj(hn, w_in_b, l, tp, 512)

        mu = rwkv_mu[l]
        vres = None
        if l > 0:
            vres = (v_first, row(rwkv_v0[l - 1]),
                    jnp.pad(rwkv_v1[l - 1], ((0, 0), (0, LANES - rwkv_v1.shape[-1]))).astype(BF16),
                    _pad_rows(rwkv_v2[l - 1], LANES).astype(BF16))
        res = _rwkv_mix(p_all, mu[:3 * D].reshape(3, D), row(mu[3 * D:]), rwkv_vec[l], row(rwkv_rk[l]),
                        _pad_rows(rwkv_w2[l], LANES).astype(BF16),
                        _pad_rows(rwkv_a2[l], LANES, front=64).astype(BF16), rwkv_g2[l].astype(BF16), vres)
        if l == 0:
            y_a, v_first = res
        else:
            (y_a,) = res

        lanes16 = lambda a: jnp.pad(row(a), ((0, 0), (0, LANES - a.shape[0])))
        y_b = _ssd_mix(p_all, pad, ssm_conv_w[l], row(ssm_conv_b[l]), lanes16(ssm_dt_bias[l]),
                       lanes16(ssm_a_log[l]), row(jnp.repeat(ssm_d[l], HEAD)), row(ssm_norm[l]))
        y_c = _ret_mix(p_all, cos_t, sin_t)
        y_d = _lru_mix(p_all, pad, lru_conv_w[l], row(lru_conv_b[l]), lru_w_gate[l].astype(BF16), lru_b_gate[l],
                       row(lru_lambda[l]))

        s = _merge(s, (y_a, y_b, y_c, y_d), p_all, pad, w_branch_b, w_out_b, l, tm_mid)
        if l + 1 < depth:
            s, hn = _ffn(s, pad, row(norm_ffn[l]), w_ffn_in_b, w_ffn_out_b, l, row(norm_mix[l + 1]), False, tm_big)
        else:
            s = _ffn(s, pad, row(norm_ffn[l]), w_ffn_in_b, w_ffn_out_b, l, row(final_norm), True, tm_big)
    return s[:, pad + N_META:]
```

```python
import functools
import math

import jax
import jax.numpy as jnp
from jax import lax
from jax.experimental import pallas as pl
from jax.experimental.pallas import tpu as pltpu

D = 1024
N_META = 16
NORM_EPS = 1e-6
CONV_K = 4
N_BRANCH = 4
CHUNK = 128
RWKV_CHUNK = 64
HEAD = 64
RWKV_LN_EPS = 64e-5
SSM_GROUPS = 4
SSM_STATE = 128
SSM_HEADS = 16
RET_HEADS = 8
RET_DK = 64
RET_DV = 128
ROPE_BASE = 10000.0
LRU_BLOCKS = 8
LRU_C = 8.0
FFN_HIDDEN = 2816
LANES = 128
SUBLANES = 8
VMEM_LIMIT = 56 * 1024 * 1024

COL_RWKV = 0
COL_Z = 3072
COL_XBC = 4096
COL_RET_QK = 6144
COL_RET_V = 7168
COL_RET_G = 8192
COL_LRU = 9216
COL_GATE = 11264
COL_RWKV_LR = 15360
COL_DT = 15616
IN_COLS = 15872

BF16 = jnp.bfloat16
F32 = jnp.float32


def _row_tile(tp, target, multiple=SUBLANES):
    best = multiple
    for t in range(multiple, target + 1, multiple):
        if tp % t == 0:
            best = t
    return best


def _cp(*sem):
    return pltpu.CompilerParams(dimension_semantics=sem, vmem_limit_bytes=VMEM_LIMIT)


def _iota(shape, dim):
    return lax.broadcasted_iota(jnp.int32, shape, dim)


def _dot(a, b):
    return jnp.dot(a.astype(BF16), b.astype(BF16), preferred_element_type=F32)


def _dot_nt(a, b):
    return lax.dot_general(a.astype(BF16), b.astype(BF16), (((1,), (1,)), ((), ())), preferred_element_type=F32)


def _dot_tn(a, b):
    return lax.dot_general(a.astype(BF16), b.astype(BF16), (((0,), (0,)), ((), ())), preferred_element_type=F32)


def _split_bf16(x):
    hi = x.astype(BF16)
    return hi, (x - hi.astype(F32)).astype(BF16)


def _expand_heads(xs, expand):
    his, los = zip(*[_split_bf16(x) for x in xs])
    out = jnp.dot(jnp.concatenate(his + los, axis=0), expand, preferred_element_type=F32)
    half = out.shape[0] // 2
    res, row = [], 0
    for x in xs:
        res.append(out[row:row + x.shape[0]] + out[half + row:half + row + x.shape[0]])
        row += x.shape[0]
    return res


def _softplus(x):
    return jnp.maximum(x, 0.0) + jnp.log1p(jnp.exp(-jnp.abs(x)))


def _silu(x):
    return x * jax.nn.sigmoid(x)


def _shift_rows(x, d, tail):
    xs = pltpu.roll(x, d, 0)
    ts = pltpu.roll(tail, d, 0)
    head = jnp.where(_iota(ts.shape, 0) < d, ts, xs[:SUBLANES])
    return jnp.concatenate([head, xs[SUBLANES:]], axis=0)


def _causal_conv(x, tail, w_ref, b_ref):
    acc = b_ref[...] + x * w_ref[CONV_K - 1:CONV_K, :]
    for d in range(1, CONV_K):
        acc = acc + _shift_rows(x, d, tail) * w_ref[CONV_K - 1 - d:CONV_K - d, :]
    return acc


def _scan_rows(a, u, carry):
    row = _iota((SUBLANES, u.shape[1]), 0)
    out = []
    for g in range(u.shape[0] // SUBLANES):
        rows = slice(g * SUBLANES, (g + 1) * SUBLANES)
        ug = u[rows]
        ag = None if a is None else a[rows]
        d = 1
        while d < SUBLANES:
            keep = row >= d
            u_prev = jnp.where(keep, pltpu.roll(ug, d, 0), 0.0)
            if ag is None:
                ug = ug + u_prev
            else:
                ug = ag * u_prev + ug
                ag = ag * jnp.where(keep, pltpu.roll(ag, d, 0), 1.0)
            d *= 2
        hg = ug + carry if ag is None else ug + ag * carry
        carry = hg[SUBLANES - 1:SUBLANES, :]
        out.append(hg)
    return jnp.concatenate(out, axis=0)


def _cumsum_rows(x):
    return _scan_rows(None, x, jnp.zeros((1, x.shape[1]), F32))


class _Rows:
    def __init__(self, ref, rows, n_rows):
        self.ref, self.rows = ref, rows
        self.shape = (n_rows, ref.shape[1])
        self.dtype = ref.dtype

    def __getitem__(self, idx):
        assert idx is Ellipsis
        return self.ref[self.rows, :]

    def __setitem__(self, idx, value):
        assert idx is Ellipsis
        self.ref[self.rows, :] = value


def _chunk_loop(chunk_fn, sub, n_in, n_param, n_out, *refs):
    ins, params = refs[:n_in], refs[n_in:n_in + n_param]
    outs, scratch = refs[n_in + n_param:n_in + n_param + n_out], refs[n_in + n_param + n_out:]
    block = pl.program_id(1)

    @pl.when(block == 0)
    def _():
        for r in scratch:
            r[...] = jnp.zeros_like(r)

    n = ins[0].shape[0] // sub

    def body(i, carry):
        rows = pl.ds(pl.multiple_of(i * sub, sub), sub)
        view = lambda r: _Rows(r, rows, sub)
        chunk_fn(block * n + i, *map(view, ins), *params, *map(view, outs), *scratch)
        return carry

    lax.fori_loop(0, n, body, 0)


def _mixer_block(tp, sub):
    return 3 * sub if tp % (3 * sub) == 0 else sub


def _valid_rows(shape, chunk_idx, pad):
    return _iota(shape, 0) + chunk_idx * shape[0] >= pad


def _rms_norm_to(hn_ref, x_ref, nw_ref):
    tm = x_ref.shape[0]
    sub = _row_tile(tm, 512, 2 * SUBLANES)

    def body(i, carry):
        rows = pl.ds(pl.multiple_of(i * sub, 2 * SUBLANES), sub)
        x = x_ref[rows, :]
        y = x * lax.rsqrt(jnp.mean(x * x, axis=-1, keepdims=True) + NORM_EPS)
        hn_ref[rows, :] = (y * nw_ref[...]).astype(BF16)
        return carry

    lax.fori_loop(0, tm // sub, body, 0)


def _inproj_kernel(x_ref, w_ref, o_ref):
    o_ref[...] = jnp.dot(x_ref[...], w_ref[...], preferred_element_type=F32)


def _inproj(hn, w, layer, tm, tn):
    b, tp, d = hn.shape
    n = w.shape[2]
    return pl.pallas_call(
        _inproj_kernel,
        grid=(b, tp // tm, n // tn),
        in_specs=[
            pl.BlockSpec((None, tm, d), lambda bi, i, j: (bi, i, 0)),
            pl.BlockSpec((None, d, tn), lambda bi, i, j: (layer, 0, j)),
        ],
        out_specs=pl.BlockSpec((None, tm, tn), lambda bi, i, j: (bi, i, j)),
        out_shape=jax.ShapeDtypeStruct((b, tp, n), F32),
        compiler_params=_cp("parallel", "parallel", "arbitrary"),
        name="inproj",
    )(hn, w)


def _lru_chunk(pad, c, y_ref, x_ref, cw_ref, cb_ref, wg_ref, bg_ref, lam_ref, o_ref, tail_ref, h_ref):
    x = x_ref[...]
    n_rows = x.shape[0]
    xc = _causal_conv(x, tail_ref[...], cw_ref, cb_ref)
    tail_ref[...] = x[n_rows - SUBLANES:]
    xcb = xc.astype(BF16)
    blk = D // LRU_BLOCKS

    def gates(k):
        parts = [jnp.dot(xcb[:, n * blk:(n + 1) * blk], wg_ref[k, n], preferred_element_type=F32)
                 for n in range(LRU_BLOCKS)]
        return jax.nn.sigmoid(jnp.concatenate(parts, axis=1) + bg_ref[k:k + 1, :])

    r_gate = gates(0)
    i_gate = gates(1)
    log_a = -LRU_C * r_gate * _softplus(-lam_ref[...])
    a = jnp.exp(log_a)
    u = jnp.sqrt(1.0 - jnp.exp(2.0 * log_a)) * (i_gate * xc)
    u = jnp.where(_valid_rows(u.shape, c, pad), u, 0.0)
    h = _scan_rows(a, u, h_ref[SUBLANES - 1:SUBLANES, :])
    h_ref[...] = h[n_rows - SUBLANES:]
    o_ref[...] = (h * jax.nn.gelu(y_ref[...])).astype(o_ref.dtype)


def _lru_mix(p_all, pad, cw, cb, wg, bg, lam):
    b, tp, _ = p_all.shape
    blk = _mixer_block(tp, CHUNK)
    full = lambda a: pl.BlockSpec(a.shape, lambda bi, c: (0,) * a.ndim)
    return pl.pallas_call(
        functools.partial(_chunk_loop, functools.partial(_lru_chunk, pad), CHUNK, 2, 5, 1),
        grid=(b, tp // blk),
        in_specs=[
            pl.BlockSpec((None, blk, D), lambda bi, c: (bi, c, COL_LRU // D)),
            pl.BlockSpec((None, blk, D), lambda bi, c: (bi, c, COL_LRU // D + 1)),
            full(cw), full(cb), full(wg), full(bg), full(lam),
        ],
        out_specs=pl.BlockSpec((None, blk, D), lambda bi, c: (bi, c, 0)),
        out_shape=jax.ShapeDtypeStruct((b, tp, D), BF16),
        scratch_shapes=[pltpu.VMEM((SUBLANES, D), F32), pltpu.VMEM((SUBLANES, D), F32)],
        compiler_params=_cp("parallel", "arbitrary"),
        name="rglru",
    )(p_all, p_all, cw, cb, wg, bg, lam)


def _ssd_chunk(pad, c, xbc_ref, z_ref, dt_ref, cw_ref, cb_ref, dtb_ref, alog_ref, dskip_ref, nw_ref,
               o_ref, tail_ref, state_ref):
    raw = xbc_ref[...]
    n_rows = raw.shape[0]
    xbc = _silu(_causal_conv(raw, tail_ref[...], cw_ref, cb_ref))
    tail_ref[...] = raw[n_rows - SUBLANES:]
    gn = SSM_GROUPS * SSM_STATE
    xs = jnp.where(_valid_rows((n_rows, D), c, pad), xbc[:, :D], 0.0)
    bm = xbc[:, D:D + gn]
    cm = xbc[:, D + gn:]

    dt = _softplus(dt_ref[...] + dtb_ref[...])
    a_cum = _cumsum_rows(dt * -jnp.exp(alog_ref[...]))
    a_last = a_cum[n_rows - 1:n_rows, :]
    expand = (_iota((LANES, D), 1) // HEAD == _iota((LANES, D), 0)).astype(BF16)
    dt_x, ecum_x, eend_x, elast_x = _expand_heads(
        [dt, jnp.exp(a_cum), jnp.exp(a_last - a_cum), jnp.broadcast_to(jnp.exp(a_last), (2 * SUBLANES, LANES))],
        expand)
    xdt = xs * dt_x
    xdt_end = xdt * eend_x
    a_cum_t = a_cum.T

    tril = _iota((n_rows, n_rows), 0) >= _iota((n_rows, n_rows), 1)
    lane_lo = _iota((n_rows, LANES), 1) < HEAD
    heads_per_group = SSM_HEADS // SSM_GROUPS
    gw = heads_per_group * HEAD
    outs = []
    for g in range(SSM_GROUPS):
        b_g = bm[:, g * SSM_STATE:(g + 1) * SSM_STATE]
        c_g = cm[:, g * SSM_STATE:(g + 1) * SSM_STATE]
        cb = _dot_nt(c_g, b_g)
        s_prev = state_ref[g]
        y_off = _dot(c_g, s_prev) * ecum_x[:, g * gw:(g + 1) * gw]
        state_ref[g] = s_prev * elast_x[:1, g * gw:(g + 1) * gw] + _dot_tn(b_g, xdt_end[:, g * gw:(g + 1) * gw])
        y_diag = []
        for pr in range(heads_per_group // 2):
            col = g * gw + pr * LANES
            x_pair = xdt[:, col:col + LANES]
            acc = None
            for e in range(2):
                h = (col // HEAD) + e
                seg = jnp.where(tril, a_cum[:, h:h + 1] - a_cum_t[h:h + 1, :], -1e30)
                m = cb * jnp.exp(seg)
                x_h = jnp.where(lane_lo if e == 0 else ~lane_lo, x_pair, 0.0)
                t = _dot(m, x_h)
                acc = t if acc is None else acc + t
            y_diag.append(acc)
        outs.append(jnp.concatenate(y_diag, axis=1) + y_off)
    y = jnp.concatenate(outs, axis=1) + xs * dskip_ref[...]
    y = y * _silu(z_ref[...])
    normed = []
    for g in range(SSM_GROUPS):
        yg = y[:, g * gw:(g + 1) * gw]
        normed.append(yg * lax.rsqrt(jnp.mean(yg * yg, axis=-1, keepdims=True) + 1e-5))
    o_ref[...] = (jnp.concatenate(normed, axis=1) * nw_ref[...]).astype(o_ref.dtype)


def _ssd_mix(p_all, pad, cw, cb, dtb, alog, dskip, nw):
    b, tp, _ = p_all.shape
    blk = _mixer_block(tp, CHUNK)
    full = lambda a: pl.BlockSpec(a.shape, lambda bi, c: (0,) * a.ndim)
    return pl.pallas_call(
        functools.partial(_chunk_loop, functools.partial(_ssd_chunk, pad), CHUNK, 3, 6, 1),
        grid=(b, tp // blk),
        in_specs=[
            pl.BlockSpec((None, blk, 2 * D), lambda bi, c: (bi, c, COL_XBC // (2 * D))),
            pl.BlockSpec((None, blk, D), lambda bi, c: (bi, c, COL_Z // D)),
            pl.BlockSpec((None, blk, LANES), lambda bi, c: (bi, c, COL_DT // LANES)),
            full(cw), full(cb), full(dtb), full(alog), full(dskip), full(nw),
        ],
        out_specs=pl.BlockSpec((None, blk, D), lambda bi, c: (bi, c, 0)),
        out_shape=jax.ShapeDtypeStruct((b, tp, D), BF16),
        scratch_shapes=[pltpu.VMEM((SUBLANES, 2 * D), F32),
                        pltpu.VMEM((SSM_GROUPS, SSM_STATE, D // SSM_GROUPS), F32)],
        compiler_params=_cp("parallel", "arbitrary"),
        name="ssd",
    )(p_all, p_all, p_all, cw, cb, dtb, alog, dskip, nw)


_RET_LOG_G = [math.log1p(-2.0 ** (-5.0 - h)) for h in range(RET_HEADS)]


def _ret_chunk(c, qk_ref, v_ref, g_ref, cos_ref, sin_ref, o_ref, state_ref):
    qk = qk_ref[...]
    n_rows, width = qk.shape
    half = RET_DK // 2
    first = (_iota(qk.shape, 1) % RET_DK) < half
    partner = jnp.where(first, pltpu.roll(qk, width - half, 1), pltpu.roll(qk, half, 1))
    qk = qk * cos_ref[...] + partner * sin_ref[...]
    hw = RET_HEADS * RET_DK
    q = qk[:, :hw]
    k = qk[:, hw:] * (RET_DK ** -0.5)
    v = v_ref[...]

    rowf = _iota((n_rows, LANES), 0).astype(F32)
    rel = (_iota((n_rows, n_rows), 0) - _iota((n_rows, n_rows), 1)).astype(F32)
    lane_lo = _iota((n_rows, LANES), 1) < RET_DK
    outs = []
    for h in range(RET_HEADS):
        log_g = _RET_LOG_G[h]
        col = (h // 2) * LANES
        mask = lane_lo if h % 2 == 0 else ~lane_lo
        q_h = jnp.where(mask, q[:, col:col + LANES], 0.0)
        k_h = jnp.where(mask, k[:, col:col + LANES], 0.0)
        v_h = v[:, h * RET_DV:(h + 1) * RET_DV]
        intra = jnp.where(rel >= 0, jnp.exp(jnp.maximum(rel, 0.0) * log_g), 0.0)
        y_in = _dot(_dot_nt(q_h, k_h) * intra, v_h)
        r_prev = state_ref[h]
        y_x = _dot(q_h, r_prev) * jnp.exp((rowf + 1.0) * log_g)
        k_dec = k_h * jnp.exp((n_rows - 1.0 - rowf) * log_g)
        state_ref[h] = math.exp(n_rows * log_g) * r_prev + _dot_tn(k_dec, v_h)
        y = y_in + y_x
        outs.append(y * lax.rsqrt(jnp.mean(y * y, axis=-1, keepdims=True) + NORM_EPS))
    o_ref[...] = (_silu(g_ref[...]) * jnp.concatenate(outs, axis=1)).astype(o_ref.dtype)


def _ret_mix(p_all, cos_t, sin_t):
    b, tp, _ = p_all.shape
    blk = _mixer_block(tp, CHUNK)
    return pl.pallas_call(
        functools.partial(_chunk_loop, _ret_chunk, CHUNK, 5, 0, 1),
        grid=(b, tp // blk),
        in_specs=[
            pl.BlockSpec((None, blk, D), lambda bi, c: (bi, c, COL_RET_QK // D)),
            pl.BlockSpec((None, blk, D), lambda bi, c: (bi, c, COL_RET_V // D)),
            pl.BlockSpec((None, blk, D), lambda bi, c: (bi, c, COL_RET_G // D)),
            pl.BlockSpec((blk, D), lambda bi, c: (c, 0)),
            pl.BlockSpec((blk, D), lambda bi, c: (c, 0)),
        ],
        out_specs=pl.BlockSpec((None, blk, D), lambda bi, c: (bi, c, 0)),
        out_shape=jax.ShapeDtypeStruct((b, tp, D), BF16),
        scratch_shapes=[pltpu.VMEM((RET_HEADS, LANES, RET_DV), F32)],
        compiler_params=_cp("parallel", "arbitrary"),
        name="retention",
    )(p_all, p_all, p_all, cos_t, sin_t)


def _head_sums(xs, ones_bd):
    n_rows, width = xs[0].shape
    tiles = width // LANES
    parts = []
    for x in xs:
        for t in _split_bf16(x):
            parts += [t[:, j * LANES:(j + 1) * LANES] for j in range(tiles)]
    sums = jnp.dot(jnp.concatenate(parts, axis=0), ones_bd, preferred_element_type=F32)
    out = []
    for i in range(len(xs)):
        base = i * 2 * tiles
        cols = [sums[(base + j) * n_rows:(base + j + 1) * n_rows] +
                sums[(base + tiles + j) * n_rows:(base + tiles + j + 1) * n_rows] for j in range(tiles)]
        out.append(jnp.concatenate(cols, axis=1))
    return out


def _drain(gen):
    while True:
        try:
            next(gen)
        except StopIteration as stop:
            return stop.value


def _interleave(main, side):
    side_value, main_done, side_done = None, False, side is None
    while not (main_done and side_done):
        if not main_done:
            try:
                next(main)
            except StopIteration:
                main_done = True
        if not side_done:
            try:
                next(side)
            except StopIteration as stop:
                side_done, side_value = True, stop.value
    return side_value


def _rwkv_kernel(has_vres, sub, *refs):
    if has_vres:
        (r_ref, k_ref, v_ref, lr_ref, vf_ref, mu_ref, mulr_ref, vec_ref, rk_ref, w2_ref, a2_ref, g2_ref,
         v0_ref, v1_ref, v2_ref, o_ref, tail_ref, taillr_ref, state_ref) = refs
    else:
        (r_ref, k_ref, v_ref, lr_ref, mu_ref, mulr_ref, vec_ref, rk_ref, w2_ref, a2_ref, g2_ref,
         o_ref, vown_ref, tail_ref, taillr_ref, state_ref) = refs

    @pl.when(pl.program_id(1) == 0)
    def _():
        for ref in (tail_ref, taillr_ref, state_ref):
            ref[...] = jnp.zeros_like(ref)

    n_rows = sub
    block_rows = r_ref.shape[0]
    w0, a0, k_k, k_a, ln_w, ln_b = (vec_ref[i:i + 1, :] for i in range(6))
    ones_bd = (_iota((LANES, LANES), 0) // HEAD == _iota((LANES, LANES), 1) // HEAD).astype(BF16)
    row4 = _iota((4 * n_rows, 2 * n_rows), 0)
    col4 = _iota((4 * n_rows, 2 * n_rows), 1)
    score_mask = (row4 % n_rows) + jnp.where(row4 >= 2 * n_rows, 1, 0) > (col4 % n_rows)
    lane_lo = _iota((n_rows, LANES), 1) < HEAD
    eye = (_iota((LANES, LANES), 0) == _iota((LANES, LANES), 1)).astype(F32)
    blockdiag = (_iota((LANES, LANES), 0) // HEAD) == (_iota((LANES, LANES), 1) // HEAD)
    in_k = _iota((2 * n_rows, 2 * n_rows), 1) < n_rows
    zeros_half = jnp.zeros((n_rows, LANES), F32)
    pairs = range(D // LANES)
    sls = [slice(j * LANES, (j + 1) * LANES) for j in pairs]

    def prepare(i):
        lo = i * sub

        def mixed(ref, tail, mu):
            x = ref[lo:lo + sub, :]
            prev = tail if i == 0 else ref[lo - SUBLANES:lo, :]
            return x + (_shift_rows(x, 1, prev) - x) * mu

        lr = mixed(lr_ref, taillr_ref[...], mulr_ref[...])
        wa = lr[:, :LANES]
        w_log = -_softplus(-(w0 + _dot(jnp.tanh(wa), w2_ref[...]))) - 0.5
        log_decay = -jnp.exp(w_log)
        yield
        cum = _cumsum_rows(log_decay)
        cum_last = cum[n_rows - 1:n_rows, :]
        yield
        a_sig = jax.nn.sigmoid(a0 + _dot(wa, a2_ref[...]))
        gate = _dot(jax.nn.sigmoid(lr[:, LANES:]), g2_ref[...])
        yield
        r = mixed(r_ref, tail_ref[0], mu_ref[0:1, :])
        k = mixed(k_ref, tail_ref[1], mu_ref[1:2, :])
        yield
        v = mixed(v_ref, tail_ref[2], mu_ref[2:3, :])
        if has_vres:
            mixv = jax.nn.sigmoid(v0_ref[...] + _dot(_dot(v, v1_ref[...]), v2_ref[...]))
            v = v + (vf_ref[lo:lo + sub, :] - v) * mixv
        else:
            vown_ref[lo:lo + sub, :] = v
        yield
        kk = k * k_k
        k = k * (1.0 + (a_sig - 1.0) * k_a)
        kk_sq, bonus_rk = _head_sums([kk * kk, r * k * rk_ref[...]], ones_bd)
        yield
        kk = kk / jnp.maximum(jnp.sqrt(kk_sq), 1e-12)
        a_vec = -kk
        b_vec = kk * a_sig
        yield
        g_inv = jnp.exp(-cum)
        to_end = jnp.exp(cum_last - cum)
        yield
        return dict(
            r_t=r * jnp.exp(cum), k_t=k * g_inv, b_t=b_vec * g_inv, a_t=a_vec * jnp.exp(cum - log_decay),
            k_end=k * to_end, b_end=b_vec * to_end, v=v, gate=gate, bonus=bonus_rk * v,
            decay_all=jnp.broadcast_to(jnp.exp(cum_last), (LANES, D)))

    def core(i, p):
        a_t, r_t, k_t, b_t, v = p["a_t"], p["r_t"], p["k_t"], p["b_t"], p["v"]
        ms, invs, powers = [], [], []
        for sl in sls:
            at, rt = a_t[:, sl], r_t[:, sl]
            lhs = jnp.concatenate([jnp.where(lane_lo, at, 0.0), jnp.where(lane_lo, 0.0, at),
                                   jnp.where(lane_lo, rt, 0.0), jnp.where(lane_lo, 0.0, rt)], axis=0)
            scores = _dot_nt(lhs, jnp.concatenate([k_t[:, sl], b_t[:, sl]], axis=0))
            ms.append(jnp.where(score_mask, scores, 0.0))
        yield
        for m in ms:
            a_ab = jnp.where(blockdiag,
                             jnp.concatenate([pltpu.roll(m[:n_rows], HEAD, 1), m[n_rows:2 * n_rows]], axis=0), 0.0)
            invs.append(eye + a_ab)
            powers.append(a_ab)
        yield
        powers = [_dot(x, x) for x in powers]
        span = 2
        while 2 * span < n_rows:
            yield
            both = [_dot(jnp.concatenate([x, q], axis=0), q) for x, q in zip(invs, powers)]
            invs = [x + b[:LANES] for x, b in zip(invs, both)]
            powers = [b[LANES:] for b in both]
            span *= 2
        yield
        invs = [x + _dot(x, q) for x, q in zip(invs, powers)]
        yield
        s_prevs = [state_ref[j] for j in pairs]
        projs = [_dot(jnp.concatenate([a_t[:, sl], r_t[:, sl]], axis=0), s) for sl, s in zip(sls, s_prevs)]
        mvs = [_dot(jnp.where(in_k, m[:2 * n_rows], 0.0), jnp.concatenate([v[:, sl], zeros_half], axis=0))
               for sl, m in zip(sls, ms)]
        yield
        u_sts = [_dot(x, jnp.where(blockdiag, jnp.concatenate([pj[:n_rows], pj[:n_rows]], axis=0) + mv, 0.0))
                 for x, pj, mv in zip(invs, projs, mvs)]
        yield
        ys = []
        for j, sl, m, pj, u_st, s_prev in zip(pairs, sls, ms, projs, u_sts, s_prevs):
            vp = v[:, sl]
            u0, u1 = u_st[:n_rows], u_st[n_rows:]
            vu = jnp.concatenate([jnp.where(lane_lo, vp, 0.0), u0, jnp.where(lane_lo, 0.0, vp), u1], axis=0)
            m_r = jnp.concatenate([m[2 * n_rows:3 * n_rows], m[3 * n_rows:]], axis=1)
            ys.append(pj[n_rows:] + _dot(m_r, vu))
            upd = _dot_tn(jnp.concatenate([p["k_end"][:, sl], p["b_end"][:, sl]], axis=0),
                          jnp.concatenate([vp, u0 + u1], axis=0))
            state_ref[j] = p["decay_all"][:, sl].T * s_prev + jnp.where(blockdiag, upd, 0.0)
        y = jnp.concatenate(ys, axis=1)
        yield
        yc = y - _head_sums([y], ones_bd)[0] * (1.0 / HEAD)
        yield
        var = _head_sums([yc * yc], ones_bd)[0] * (1.0 / HEAD)
        y = yc * lax.rsqrt(var + RWKV_LN_EPS) * ln_w + ln_b
        o_ref[i * sub:(i + 1) * sub, :] = ((y + p["bonus"]) * p["gate"]).astype(o_ref.dtype)

    prepared = _drain(prepare(0))
    for i in range(block_rows // sub):
        following = prepare(i + 1) if (i + 1) * sub < block_rows else None
        prepared = _interleave(core(i, prepared), following)
    for n, ref in enumerate((r_ref, k_ref, v_ref)):
        tail_ref[n] = ref[block_rows - SUBLANES:, :]
    taillr_ref[...] = lr_ref[block_rows - SUBLANES:, :]


def _rwkv_mix(p_all, mu, mu_lr, vec, rk, w2, a2, g2, vres):
    b, tp, _ = p_all.shape
    tb = _mixer_block(tp, RWKV_CHUNK)
    full = lambda a: pl.BlockSpec(a.shape, lambda bi, c: (0,) * a.ndim)
    col = lambda off, w: pl.BlockSpec((None, tb, w), lambda bi, c: (bi, c, off // w))
    row_out = pl.BlockSpec((None, tb, D), lambda bi, c: (bi, c, 0))
    blocks = [col(COL_RWKV, D), col(COL_RWKV + D, D), col(COL_RWKV + 2 * D, D), col(COL_RWKV_LR, 2 * LANES)]
    block_args = [p_all, p_all, p_all, p_all]
    params = [mu, mu_lr, vec, rk, w2, a2, g2]
    scratch = [pltpu.VMEM((3, SUBLANES, D), F32), pltpu.VMEM((SUBLANES, 2 * LANES), F32),
               pltpu.VMEM((D // LANES, LANES, LANES), F32)]
    if vres is None:
        out_specs = [row_out, row_out]
        out_shape = [jax.ShapeDtypeStruct((b, tp, D), BF16), jax.ShapeDtypeStruct((b, tp, D), F32)]
    else:
        v_first, v0, v1, v2 = vres
        blocks.append(row_out)
        block_args.append(v_first)
        params += [v0, v1, v2]
        out_specs = [row_out]
        out_shape = [jax.ShapeDtypeStruct((b, tp, D), BF16)]
    in_specs = blocks + [full(a) for a in params]
    args = block_args + params
    return pl.pallas_call(
        functools.partial(_rwkv_kernel, vres is not None, RWKV_CHUNK),
        grid=(b, tp // tb),
        in_specs=in_specs,
        out_specs=out_specs,
        out_shape=out_shape,
        scratch_shapes=scratch,
        compiler_params=_cp("parallel", "arbitrary"),
        name="rwkv7",
    )(*args)


def _merge_kernel(pad, s_ref, ya_ref, yb_ref, yc_ref, yd_ref, g0_ref, g1_ref, g2_ref, g3_ref, wb_ref, wo_ref,
                  o_ref):
    acc = None
    for n, (y_ref, g_ref) in enumerate(((ya_ref, g0_ref), (yb_ref, g1_ref), (yc_ref, g2_ref), (yd_ref, g3_ref))):
        t = jax.nn.sigmoid(g_ref[...]) * jnp.dot(y_ref[...], wb_ref[n], preferred_element_type=F32)
        acc = t if acc is None else acc + t
    out = s_ref[...] + _dot(acc, wo_ref[...])
    o_ref[...] = jnp.where(_valid_rows(out.shape, pl.program_id(1), pad), out, 0.0)


def _merge(s, ys, p_all, pad, wb, wo, layer, tm):
    b, tp, d = s.shape
    row = pl.BlockSpec((None, tm, d), lambda bi, i: (bi, i, 0))
    gate = lambda n: pl.BlockSpec((None, tm, d), lambda bi, i: (bi, i, COL_GATE // D + n))
    return pl.pallas_call(
        functools.partial(_merge_kernel, pad),
        grid=(b, tp // tm),
        in_specs=[row, row, row, row, row, gate(0), gate(1), gate(2), gate(3),
                  pl.BlockSpec((None,) + wb.shape[1:], lambda bi, i: (layer, 0, 0, 0), pipeline_mode=pl.Buffered(1)),
                  pl.BlockSpec((None,) + wo.shape[1:], lambda bi, i: (layer, 0, 0), pipeline_mode=pl.Buffered(1))],
        out_specs=row,
        out_shape=jax.ShapeDtypeStruct(s.shape, F32),
        compiler_params=_cp("parallel", "parallel"),
        name="merge",
    )(s, *ys, p_all, p_all, p_all, p_all, wb, wo)


FFN_COLS = 256


def _ffn_kernel(pad, last, s_ref, nw_ref, wi_ref, wo_ref, nxt_ref, *refs):
    if last:
        o_ref, hn_ref, acc_ref = refs
    else:
        o_ref, hn_out_ref, hn_ref, acc_ref = refs
    row_tile = pl.program_id(1)
    tm = s_ref.shape[0]
    hidden = wo_ref.shape[0]
    _rms_norm_to(hn_ref, s_ref, nw_ref)
    for c in range(hidden // FFN_COLS):
        cols = slice(c * FFN_COLS, (c + 1) * FFN_COLS)
        hn = hn_ref[...]
        g = jnp.dot(hn, wi_ref[:, cols], preferred_element_type=F32)
        u = jnp.dot(hn, wi_ref[:, hidden + c * FFN_COLS:hidden + (c + 1) * FFN_COLS], preferred_element_type=F32)
        t = _dot(_silu(g) * u, wo_ref[cols, :])
        if c == 0:
            acc_ref[...] = t
        else:
            acc_ref[...] += t

    sub = _row_tile(tm, 512, 2 * SUBLANES)

    def body(i, carry):
        rows = pl.ds(pl.multiple_of(i * sub, 2 * SUBLANES), sub)
        out = s_ref[rows, :] + acc_ref[rows, :]
        valid = _iota(out.shape, 0) + (row_tile * tm + i * sub) >= pad
        out = jnp.where(valid, out, 0.0)
        normed = out * lax.rsqrt(jnp.mean(out * out, axis=-1, keepdims=True) + NORM_EPS) * nxt_ref[...]
        if last:
            o_ref[rows, :] = normed
        else:
            o_ref[rows, :] = out
            hn_out_ref[rows, :] = normed.astype(BF16)
        return carry

    lax.fori_loop(0, tm // sub, body, 0)


def _ffn(s, pad, nw, w_in, w_out, layer, nxt_w, last, tm):
    b, tp, d = s.shape
    row = pl.BlockSpec((None, tm, d), lambda bi, i: (bi, i, 0))
    vec = pl.BlockSpec((1, d), lambda bi, i: (0, 0))
    resident = lambda w: pl.BlockSpec((None,) + w.shape[1:], lambda bi, i: (layer, 0, 0),
                                      pipeline_mode=pl.Buffered(1))
    out_specs, out_shape = row, jax.ShapeDtypeStruct(s.shape, F32)
    if not last:
        out_specs, out_shape = [row, row], [out_shape, jax.ShapeDtypeStruct(s.shape, BF16)]
    return pl.pallas_call(
        functools.partial(_ffn_kernel, pad, last),
        grid=(b, tp // tm),
        in_specs=[row, vec, resident(w_in), resident(w_out), vec],
        out_specs=out_specs,
        out_shape=out_shape,
        scratch_shapes=[pltpu.VMEM((tm, d), BF16), pltpu.VMEM((tm, d), F32)],
        compiler_params=_cp("parallel", "parallel"),
        name="ffn",
    )(s, nw, w_in, w_out, nxt_w)


def _norm_kernel(x_ref, w_ref, o_ref):
    _rms_norm_to(o_ref, x_ref, w_ref)


def _norm_bf16(s, w, tm):
    b, tp, d = s.shape
    row = pl.BlockSpec((None, tm, d), lambda bi, i: (bi, i, 0))
    return pl.pallas_call(
        _norm_kernel,
        grid=(b, tp // tm),
        in_specs=[row, pl.BlockSpec((1, d), lambda bi, i: (0, 0))],
        out_specs=row,
        out_shape=jax.ShapeDtypeStruct(s.shape, BF16),
        compiler_params=_cp("parallel", "parallel"),
        name="first_norm",
    )(s, w)


def _pad_rows(a, rows, front=0):
    return jnp.pad(a, ((front, rows - front - a.shape[0]), (0, 0)))


def _in_weight(w):
    lr0, lr1 = 3 * D, 3 * D + 2 * LANES
    dt0 = lr1 + 3 * D
    dt1 = dt0 + SSM_HEADS
    tail = jnp.zeros(w.shape[:-1] + (IN_COLS - w.shape[-1],), w.dtype)
    out = jnp.concatenate([w[..., :lr0], w[..., lr1:dt0], w[..., dt1:], w[..., lr0:lr1], w[..., dt0:dt1], tail],
                          axis=-1)
    return out.astype(BF16)


def _rope_tables(tp, pad):
    half = RET_DK // 2
    freq = jnp.power(ROPE_BASE, -jnp.arange(half, dtype=F32) / half)
    pos = jnp.arange(tp, dtype=F32) - pad
    ang = pos[:, None] * freq[None, :]
    cos = jnp.tile(jnp.concatenate([jnp.cos(ang), jnp.cos(ang)], axis=1), (1, 2 * RET_HEADS))
    sin = jnp.tile(jnp.concatenate([-jnp.sin(ang), jnp.sin(ang)], axis=1), (1, 2 * RET_HEADS))
    return cos, sin


def kernel(x, meta, norm_mix, norm_ffn, w_in, rwkv_mu, rwkv_w2, rwkv_a2, rwkv_g2, rwkv_vec, rwkv_rk, rwkv_v0,
           rwkv_v1, rwkv_v2, ssm_conv_w, ssm_conv_b, ssm_dt_bias, ssm_a_log, ssm_d, ssm_norm, lru_conv_w,
           lru_conv_b, lru_w_gate, lru_b_gate, lru_lambda, w_branch, w_out, w_ffn_in, w_ffn_out, final_norm):
    bsz, seq, d = x.shape
    depth = w_in.shape[0]
    t_len = seq + N_META
    pad = (-t_len) % CHUNK
    tp = t_len + pad
    s = jnp.concatenate([jnp.zeros((bsz, pad, d), x.dtype),
                         jnp.broadcast_to(meta[None].astype(x.dtype), (bsz, N_META, d)), x], axis=1)
    cos_t, sin_t = _rope_tables(tp, pad)
    tm_big = _row_tile(tp, 1056, 2 * SUBLANES)
    tm_mid = _row_tile(tp, 528, 2 * SUBLANES)
    row = lambda a: a.reshape(1, -1)

    w_in_b = _in_weight(w_in)
    w_branch_b, w_out_b = w_branch.astype(BF16), w_out.astype(BF16)
    w_ffn_in_b, w_ffn_out_b = w_ffn_in.astype(BF16), w_ffn_out.astype(BF16)

    v_first = None
    hn = _norm_bf16(s, row(norm_mix[0]), tm_mid)
    for l in range(depth):
        p_all = _inproj(hn, w_in_b, l, tp, 512)

        mu = rwkv_mu[l]
        vres = None
        if l > 0:
            vres = (v_first, row(rwkv_v0[l - 1]),
                    jnp.pad(rwkv_v1[l - 1], ((0, 0), (0, LANES - rwkv_v1.shape[-1]))).astype(BF16),
                    _pad_rows(rwkv_v2[l - 1], LANES).astype(BF16))
        res = _rwkv_mix(p_all, mu[:3 * D].reshape(3, D), row(mu[3 * D:]), rwkv_vec[l], row(rwkv_rk[l]),
                        _pad_rows(rwkv_w2[l], LANES).astype(BF16),
                        _pad_rows(rwkv_a2[l], LANES, front=64).astype(BF16), rwkv_g2[l].astype(BF16), vres)
        if l == 0:
            y_a, v_first = res
        else:
            (y_a,) = res

        lanes16 = lambda a: jnp.pad(row(a), ((0, 0), (0, LANES - a.shape[0])))
        y_b = _ssd_mix(p_all, pad, ssm_conv_w[l], row(ssm_conv_b[l]), lanes16(ssm_dt_bias[l]),
                       lanes16(ssm_a_log[l]), row(jnp.repeat(ssm_d[l], HEAD)), row(ssm_norm[l]))
        y_c = _ret_mix(p_all, cos_t, sin_t)
        y_d = _lru_mix(p_all, pad, lru_conv_w[l], row(lru_conv_b[l]), lru_w_gate[l].astype(BF16), lru_b_gate[l],
                       row(lru_lambda[l]))

        s = _merge(s, (y_a, y_b, y_c, y_d), p_all, pad, w_branch_b, w_out_b, l, tm_mid)
        if l + 1 < depth:
            s, hn = _ffn(s, pad, row(norm_ffn[l]), w_ffn_in_b, w_ffn_out_b, l, row(norm_mix[l + 1]), False, tm_big)
        else:
            s = _ffn(s, pad, row(norm_ffn[l]), w_ffn_in_b, w_ffn_out_b, l, row(final_norm), True, tm_big)
    return s[:, pad + N_META:]
```
